```python
import jax, jax.numpy as jnp
from jax import lax
import numpy as np

D_MODEL = 2048
BATCH = 4
SEQ = 2048
DEPTH = 4
DEC_BATCH = 128
DEC_SEQ = 1
PAST_LEN = 16384
PAGE_SIZE = 128

D_MIX = D_MODEL
CONV_WIDTH = D_MIX // 2
RET_WIDTH = D_MIX - CONV_WIDTH
CONV_K = 3
RET_HEADS = 4
RET_HEAD_DIM = RET_WIDTH // RET_HEADS
RET_CHUNK = 128
ROPE_BASE = 10000.0
PLE_DIM = 256
NORM_EPS = 1e-6
GN_EPS = 1e-6
IN_COLS = 4 * CONV_WIDTH + 4 * RET_WIDTH

kernel_name = "hybrid_conv_retention_decoder_step"


def rmsnorm(x, g):
    xf = x.astype(jnp.float32)
    y = xf * lax.rsqrt(jnp.mean(xf * xf, axis=-1, keepdims=True) + NORM_EPS)
    return (y * g.astype(jnp.float32)).astype(x.dtype)


def rope(x, pos):
    half = x.shape[-1] // 2
    inv = jnp.power(ROPE_BASE, -jnp.arange(half, dtype=jnp.float32) / half)
    ang = pos.astype(jnp.float32)[:, None] * inv[None, :]
    cos = jnp.cos(ang)[None, :, None, :]
    sin = jnp.sin(ang)[None, :, None, :]
    xf = x.astype(jnp.float32)
    x1, x2 = xf[..., :half], xf[..., half:]
    return jnp.concatenate([x1 * cos - x2 * sin, x1 * sin + x2 * cos], axis=-1)


def retention_log_gamma():
    return jnp.log1p(-jnp.exp2(-5.0 - jnp.arange(RET_HEADS, dtype=jnp.float32)))


def retention_chunk(S, q, k, v, log_gamma):
    L = q.shape[1]
    idx = jnp.arange(L, dtype=jnp.float32)
    diff = idx[:, None] - idx[None, :]
    decay = jnp.where(diff >= 0.0,
                      jnp.exp(log_gamma[:, None, None] * jnp.maximum(diff, 0.0)[None]),
                      0.0)
    scores = jnp.einsum('blhd,bmhd->bhlm', q, k) * decay[None]
    intra = jnp.einsum('bhlm,bmhe->blhe', scores, v)
    q_dec = jnp.exp(log_gamma[None, :] * (idx[:, None] + 1.0))
    inter = jnp.einsum('blhd,bhde->blhe', q, S) * q_dec[None, :, :, None]
    k_dec = jnp.exp(log_gamma[None, :] * (L - 1.0 - idx[:, None]))
    S_new = (jnp.exp(log_gamma * L)[None, :, None, None] * S
             + jnp.einsum('blhd,blhe->bhde', k * k_dec[None, :, :, None], v))
    return S_new, intra + inter


def retention(q, k, v, S0, log_gamma):
    B, L, H, Dk = q.shape
    C = RET_CHUNK if L % RET_CHUNK == 0 else L
    n = L // C

    def to_chunks(t):
        return t.reshape(B, n, C, H, t.shape[-1]).transpose(1, 0, 2, 3, 4)

    def step(S, qkv):
        qc, kc, vc = qkv
        return retention_chunk(S, qc, kc, vc, log_gamma)

    S_final, o = lax.scan(step, S0, (to_chunks(q), to_chunks(k), to_chunks(v)))
    o = o.transpose(1, 0, 2, 3, 4).reshape(B, L, H, v.shape[-1])
    return o, S_final


def short_conv(u, buf, w):
    L = u.shape[1]
    ext = jnp.concatenate([buf.astype(u.dtype), u], axis=1)
    y = w[0] * ext[:, 0:L]
    for j in range(1, CONV_K):
        y = y + w[j] * ext[:, j:j + L]
    return y, ext[:, -(CONV_K - 1):]


def split_projection(z):
    widths = [CONV_WIDTH] * 4 + [RET_WIDTH] * 4
    points = [int(s) for s in np.cumsum(widths)[:-1]]
    return jnp.split(z, points, axis=-1)


def trunk_layer(x, p_i, pos, conv_buf, ret_S, norm_g, w_in, conv_w, gn_g, w_out, w_pg, w_ple):
    Bsz, L, _ = x.shape
    h = rmsnorm(x, norm_g)
    z = h @ w_in
    b_c, c_c, x_c, g_c, q, k, v, g_r = split_projection(z)
    conv_y, new_buf = short_conv(c_c * x_c, conv_buf, conv_w)
    y_conv = b_c * conv_y * jax.nn.silu(g_c)
    q = rope(q.reshape(Bsz, L, RET_HEADS, RET_HEAD_DIM), pos)
    k = rope(k.reshape(Bsz, L, RET_HEADS, RET_HEAD_DIM), pos) * (RET_HEAD_DIM ** -0.5)
    v = v.reshape(Bsz, L, RET_HEADS, RET_HEAD_DIM).astype(jnp.float32)
    o, new_S = retention(q, k, v, ret_S.astype(jnp.float32), retention_log_gamma())
    mu = jnp.mean(o, axis=-1, keepdims=True)
    var = jnp.mean(jnp.square(o - mu), axis=-1, keepdims=True)
    o = ((o - mu) * lax.rsqrt(var + GN_EPS)).reshape(Bsz, L, RET_WIDTH)
    y_ret = (o * gn_g.astype(jnp.float32)).astype(x.dtype) * jax.nn.silu(g_r)
    x = x + jnp.concatenate([y_conv, y_ret], axis=-1) @ w_out
    x = x + jax.nn.sigmoid(x @ w_pg) * (p_i @ w_ple)
    return x, new_buf, new_S.astype(x.dtype)


def setup_inputs(seed: int = 0) -> dict:
    key = jax.random.key(seed)
    ks = jax.random.split(key, 16)
    f32 = jnp.float32
    return {
        "x_prompt": jax.random.normal(ks[0], (BATCH, SEQ, D_MODEL), f32),
        "x_sample": jax.random.normal(ks[1], (DEC_BATCH, DEC_SEQ, D_MODEL), f32),
        "state_conv": jax.random.normal(ks[2], (DEPTH, DEC_BATCH, CONV_K - 1, CONV_WIDTH), f32),
        "state_ret": 0.5 * jax.random.normal(ks[3], (DEPTH, DEC_BATCH, RET_HEADS, RET_HEAD_DIM, RET_HEAD_DIM), f32),
        "p_prompt": jax.random.normal(ks[4], (DEPTH, BATCH, SEQ, PLE_DIM), f32),
        "p_sample": jax.random.normal(ks[5], (DEPTH, DEC_BATCH, DEC_SEQ, PLE_DIM), f32),
        "norm_g": 1.0 + 0.02 * jax.random.normal(ks[6], (DEPTH, D_MODEL), f32),
        "w_in": jax.random.normal(ks[7], (DEPTH, D_MODEL, IN_COLS), f32) * D_MODEL ** -0.5,
        "conv_w": jax.random.normal(ks[8], (DEPTH, CONV_K, CONV_WIDTH), f32) * CONV_K ** -0.5,
        "gn_g": 1.0 + 0.02 * jax.random.normal(ks[9], (DEPTH, RET_WIDTH), f32),
        "w_out": jax.random.normal(ks[10], (DEPTH, D_MIX, D_MODEL), f32) * D_MIX ** -0.5,
        "w_pg": jax.random.normal(ks[11], (DEPTH, D_MODEL, D_MODEL), f32) * D_MODEL ** -0.5,
        "w_ple": jax.random.normal(ks[12], (DEPTH, PLE_DIM, D_MODEL), f32) * PLE_DIM ** -0.5,
        "final_g": 1.0 + 0.02 * jax.random.normal(ks[13], (D_MODEL,), f32),
    }


def reference(x_prompt, x_sample, state_conv, state_ret, p_prompt, p_sample,
              norm_g, w_in, conv_w, gn_g, w_out, w_pg, w_ple, final_g):
    pos_prompt = jnp.arange(SEQ, dtype=jnp.int32)
    pos_sample = PAST_LEN + jnp.arange(DEC_SEQ, dtype=jnp.int32)
    hp, hs = x_prompt, x_sample
    conv_p, ret_p, conv_s, ret_s = [], [], [], []
    for i in range(DEPTH):
        buf0 = jnp.zeros((BATCH, CONV_K - 1, CONV_WIDTH), x_prompt.dtype)
        S0 = jnp.zeros((BATCH, RET_HEADS, RET_HEAD_DIM, RET_HEAD_DIM), jnp.float32)
        hp, nb, nS = trunk_layer(hp, p_prompt[i], pos_prompt, buf0, S0,
                                 norm_g[i], w_in[i], conv_w[i], gn_g[i], w_out[i], w_pg[i], w_ple[i])
        conv_p.append(nb)
        ret_p.append(nS)
        hs, nb, nS = trunk_layer(hs, p_sample[i], pos_sample, state_conv[i], state_ret[i],
                                 norm_g[i], w_in[i], conv_w[i], gn_g[i], w_out[i], w_pg[i], w_ple[i])
        conv_s.append(nb)
        ret_s.append(nS)
    y_prompt = rmsnorm(hp, final_g)
    y_sample = rmsnorm(hs, final_g)
    return (y_prompt, y_sample, jnp.stack(conv_p), jnp.stack(ret_p), jnp.stack(conv_s), jnp.stack(ret_s))
```

```python
import functools

import jax
import jax.numpy as jnp
import numpy as np
from jax import lax
from jax.experimental import pallas as pl
from jax.experimental.pallas import tpu as pltpu

_ROPE_BASE = 10000.0
_NORM_EPS = 1e-6
_GN_EPS = 1e-6
_CONV_K = 3
_RET_HEADS = 4
_RET_CHUNK = 128

_V7X_SUBLANES = 8
_V7X_VMEM_LIMIT_BYTES = 56 * 1024 * 1024

_BF16 = jnp.bfloat16
_F32 = jnp.float32


def _mm(a, b):
    return jnp.dot(a, b, preferred_element_type=_F32)


def _mm_nt(a, b):
    return lax.dot_general(a, b, (((1,), (1,)), ((), ())), preferred_element_type=_F32)


def _rmsnorm(x, g):
    return x * lax.rsqrt(jnp.mean(x * x, axis=-1, keepdims=True) + _NORM_EPS) * g


def _rope(x, cos, sin):
    half = x.shape[-1] // 2
    x1, x2 = x[:, :half], x[:, half:]
    return jnp.concatenate([x1 * cos - x2 * sin, x1 * sin + x2 * cos], axis=-1)


def _group_norm(o):
    mu = jnp.mean(o, axis=-1, keepdims=True)
    d = o - mu
    var = jnp.mean(d * d, axis=-1, keepdims=True)
    return d * lax.rsqrt(var + _GN_EPS)


def _norm_kernel(x_ref, g_ref, h_ref):
    h_ref[...] = _rmsnorm(x_ref[...], g_ref[...]).astype(h_ref.dtype)


def _norm_call(x, g, rows):
    n, d = x.shape
    return pl.pallas_call(
        _norm_kernel,
        grid=(n // rows,),
        in_specs=[pl.BlockSpec((rows, d), lambda r: (r, 0)),
                  pl.BlockSpec((1, d), lambda r: (0, 0))],
        out_specs=pl.BlockSpec((rows, d), lambda r: (r, 0)),
        out_shape=jax.ShapeDtypeStruct((n, d), _BF16),
        compiler_params=pltpu.CompilerParams(
            dimension_semantics=("arbitrary",), vmem_limit_bytes=_V7X_VMEM_LIMIT_BYTES),
        name="rmsnorm_rows",
    )(x, g)


def _prompt_mixer_kernel(h_ref, wb_ref, wc_ref, wx_ref, wg_ref, wq_ref, wk_ref, wv_ref, wr_ref,
                         convw_ref, gng_ref, cos_ref, sin_ref, decay_ref, qdec_ref, kdec_ref,
                         gc_ref, yc_ref, yr_ref, nconv_ref, nret_ref, tail_ref):
    t = pl.program_id(2)
    nt = pl.num_programs(2)
    tm = h_ref.shape[1]
    h = h_ref[0]

    @pl.when(t == 0)
    def _():
        tail_ref[...] = jnp.zeros_like(tail_ref)
        nret_ref[...] = jnp.zeros_like(nret_ref)

    u = _mm(h, wc_ref[...]) * _mm(h, wx_ref[...])
    prev = tail_ref[...]
    row = lax.broadcasted_iota(jnp.int32, u.shape, 0)
    p1 = prev[_V7X_SUBLANES - 1:_V7X_SUBLANES]
    p2 = prev[_V7X_SUBLANES - 2:_V7X_SUBLANES - 1]
    s1 = jnp.where(row == 0, p1, pltpu.roll(u, 1, 0))
    s2 = jnp.where(row == 0, p2, jnp.where(row == 1, p1, pltpu.roll(u, 2, 0)))
    cw = convw_ref[...]
    conv = cw[0:1] * s2 + cw[1:2] * s1 + cw[2:3] * u
    yc_ref[0] = (_mm(h, wb_ref[...]) * conv * jax.nn.silu(_mm(h, wg_ref[...]))).astype(yc_ref.dtype)
    tail_ref[...] = u[tm - _V7X_SUBLANES:tm]

    @pl.when(t == nt - 1)
    def _():
        nconv_ref[0] = u[tm - (_CONV_K - 1):tm]

    zq = _mm(h, wq_ref[...])
    zk = _mm(h, wk_ref[...])
    zv = _mm(h, wv_ref[...])
    zr = _mm(h, wr_ref[...])
    decay = decay_ref[0]
    qdec = qdec_ref[0]
    kdec = kdec_ref[0]
    gc = gc_ref[0]
    gng = gng_ref[...]
    s_state = nret_ref[0, 0]
    scale = zq.shape[-1] ** -0.5
    c = _RET_CHUNK
    for j in range(tm // c):
        sl = slice(j * c, (j + 1) * c)
        cos = cos_ref[sl, :]
        sin = sin_ref[sl, :]
        qr = _rope(zq[sl], cos, sin)
        kr = _rope(zk[sl], cos, sin) * scale
        v = zv[sl].astype(_BF16)
        qb = qr.astype(_BF16)
        scores = _mm_nt(qb, kr.astype(_BF16)) * decay
        intra = _mm(scores.astype(_BF16), v)
        inter = _mm(qb, s_state.astype(_BF16)) * qdec
        s_state = gc * s_state + _mm((kr * kdec).T.astype(_BF16), v)
        on = _group_norm(intra + inter)
        yr_ref[0, sl, :] = (on * gng * jax.nn.silu(zr[sl])).astype(yr_ref.dtype)
    nret_ref[0, 0] = s_state


def _prompt_mixer_call(layer, h, w_in, conv_w, gn_g, tabs, tm):
    b, l, d = h.shape
    width = conv_w.shape[-1]
    gw = width // _RET_HEADS
    cos, sin, decay, qdec, kdec, gc = tabs

    def wspec(seg):
        return pl.BlockSpec((None, d, gw), lambda g, bi, t: (layer, 0, seg * _RET_HEADS + g))

    in_specs = [pl.BlockSpec((1, tm, d), lambda g, bi, t: (bi, t, 0))]
    in_specs += [wspec(s) for s in range(8)]
    in_specs += [
        pl.BlockSpec((None, _CONV_K, gw), lambda g, bi, t: (layer, 0, g)),
        pl.BlockSpec((None, 1, gw), lambda g, bi, t: (layer, 0, g)),
        pl.BlockSpec((tm, gw // 2), lambda g, bi, t: (t, 0)),
        pl.BlockSpec((tm, gw // 2), lambda g, bi, t: (t, 0)),
        pl.BlockSpec((1, _RET_CHUNK, _RET_CHUNK), lambda g, bi, t: (g, 0, 0)),
        pl.BlockSpec((1, _RET_CHUNK, gw), lambda g, bi, t: (g, 0, 0)),
        pl.BlockSpec((1, _RET_CHUNK, gw), lambda g, bi, t: (g, 0, 0)),
        pl.BlockSpec((1, 1, gw), lambda g, bi, t: (g, 0, 0)),
    ]
    out_specs = [
        pl.BlockSpec((1, tm, gw), lambda g, bi, t: (bi, t, g)),
        pl.BlockSpec((1, tm, gw), lambda g, bi, t: (bi, t, g)),
        pl.BlockSpec((1, _CONV_K - 1, gw), lambda g, bi, t: (bi, 0, g)),
        pl.BlockSpec((1, 1, gw, gw), lambda g, bi, t: (bi, g, 0, 0)),
    ]
    out_shape = [
        jax.ShapeDtypeStruct((b, l, width), _BF16),
        jax.ShapeDtypeStruct((b, l, width), _BF16),
        jax.ShapeDtypeStruct((b, _CONV_K - 1, width), _F32),
        jax.ShapeDtypeStruct((b, _RET_HEADS, gw, gw), _F32),
    ]
    return pl.pallas_call(
        _prompt_mixer_kernel,
        grid=(_RET_HEADS, b, l // tm),
        in_specs=in_specs,
        out_specs=out_specs,
        out_shape=out_shape,
        scratch_shapes=[pltpu.VMEM((_V7X_SUBLANES, gw), _F32)],
        compiler_params=pltpu.CompilerParams(
            dimension_semantics=("arbitrary", "arbitrary", "arbitrary"),
            vmem_limit_bytes=_V7X_VMEM_LIMIT_BYTES),
        name="prompt_mixer",
    )(h, *([w_in] * 8), conv_w, gn_g, cos, sin, decay, qdec, kdec, gc)


def _proj_kernel(h_ref, w_ref, z_ref):
    z_ref[...] = _mm(h_ref[...], w_ref[...])


def _sample_proj_call(layer, h, w_in, tn):
    n, d = h.shape
    cols = w_in.shape[-1]
    return pl.pallas_call(
        _proj_kernel,
        grid=(cols // tn,),
        in_specs=[pl.BlockSpec((n, d), lambda j: (0, 0)),
                  pl.BlockSpec((None, d, tn), lambda j: (layer, 0, j))],
        out_specs=pl.BlockSpec((n, tn), lambda j: (0, j)),
        out_shape=jax.ShapeDtypeStruct((n, cols), _F32),
        compiler_params=pltpu.CompilerParams(
            dimension_semantics=("arbitrary",), vmem_limit_bytes=_V7X_VMEM_LIMIT_BYTES),
        name="sample_proj",
    )(h, w_in)


def _sample_mixer_kernel(zb_ref, zc_ref, zx_ref, zg_ref, zq_ref, zk_ref, zv_ref, zr_ref,
                         convw_ref, gng_ref, cos_ref, sin_ref, gamma_ref, buf_ref, s_ref,
                         *rest):
    yc_ref, yr_ref, nconv_ref, nret_ref, o_ref = rest[-5:]
    bb = zq_ref.shape[0]

    u = zc_ref[...] * zx_ref[...]
    cw = convw_ref[...]
    conv = cw[0:1] * buf_ref[0] + cw[1:2] * buf_ref[1] + cw[2:3] * u
    yc_ref[...] = (zb_ref[...] * conv * jax.nn.silu(zg_ref[...])).astype(yc_ref.dtype)
    nconv_ref[0] = buf_ref[1]
    nconv_ref[1] = u

    cos = cos_ref[...]
    sin = sin_ref[...]
    qr = _rope(zq_ref[...], cos, sin)
    kr = _rope(zk_ref[...], cos, sin) * (zq_ref.shape[-1] ** -0.5)
    v = zv_ref[...]
    gamma = gamma_ref[0]
    q_t = qr.T
    k_t = kr.T
    for i in range(bb):
        s_new = gamma * s_ref[0, i, 0] + k_t[:, i:i + 1] * v[i:i + 1, :]
        nret_ref[0, i, 0] = s_new
        o_ref[i:i + 1, :] = jnp.sum(q_t[:, i:i + 1] * s_new, axis=0, keepdims=True)
    on = _group_norm(o_ref[...])
    yr_ref[...] = (on * gng_ref[...] * jax.nn.silu(zr_ref[...])).astype(yr_ref.dtype)


def _sample_mixer_call(layer, z, state_conv, state_ret, conv_w, gn_g, cos, sin, gamma, nret_prev, bb):
    n = z.shape[0]
    depth = state_ret.shape[0]
    width = conv_w.shape[-1]
    gw = width // _RET_HEADS

    def zspec(seg):
        return pl.BlockSpec((bb, gw), lambda g, bi: (bi, seg * _RET_HEADS + g))

    in_specs = [zspec(s) for s in range(8)]
    in_specs += [
        pl.BlockSpec((None, _CONV_K, gw), lambda g, bi: (layer, 0, g)),
        pl.BlockSpec((None, 1, gw), lambda g, bi: (layer, 0, g)),
        pl.BlockSpec((1, gw // 2), lambda g, bi: (0, 0)),
        pl.BlockSpec((1, gw // 2), lambda g, bi: (0, 0)),
        pl.BlockSpec((1, 1, gw), lambda g, bi: (g, 0, 0)),
        pl.BlockSpec((None, _CONV_K - 1, bb, gw), lambda g, bi: (layer, 0, bi, g)),
        pl.BlockSpec((1, bb, 1, gw, gw), lambda g, bi: (layer, bi, g, 0, 0)),
    ]
    args = [z] * 8 + [conv_w, gn_g, cos, sin, gamma, state_conv, state_ret]
    aliases = {}
    if nret_prev is not None:
        in_specs.append(pl.BlockSpec(memory_space=pl.ANY))
        args.append(nret_prev)
        aliases = {len(args) - 1: 3}
    out_specs = [
        pl.BlockSpec((bb, gw), lambda g, bi: (bi, g)),
        pl.BlockSpec((bb, gw), lambda g, bi: (bi, g)),
        pl.BlockSpec((_CONV_K - 1, bb, gw), lambda g, bi: (0, bi, g)),
        pl.BlockSpec((1, bb, 1, gw, gw), lambda g, bi: (layer, bi, g, 0, 0)),
    ]
    out_shape = [
        jax.ShapeDtypeStruct((n, width), _BF16),
        jax.ShapeDtypeStruct((n, width), _BF16),
        jax.ShapeDtypeStruct((_CONV_K - 1, n, width), _F32),
        jax.ShapeDtypeStruct((depth, n, _RET_HEADS, gw, gw), _F32),
    ]
    return pl.pallas_call(
        _sample_mixer_kernel,
        grid=(_RET_HEADS, n // bb),
        in_specs=in_specs,
        out_specs=out_specs,
        out_shape=out_shape,
        scratch_shapes=[pltpu.VMEM((bb, gw), _F32)],
        input_output_aliases=aliases,
        compiler_params=pltpu.CompilerParams(
            dimension_semantics=("arbitrary", "arbitrary"),
            vmem_limit_bytes=_V7X_VMEM_LIMIT_BYTES),
        name="sample_mixer",
    )(*args)


def _merge_kernel(x_ref, yc_ref, yr_ref, p_ref, wo_ref, wpg_ref, wple_ref, g_ref, *out_refs):
    half = yc_ref.shape[-1]
    x1 = x_ref[...] + _mm(yc_ref[...], wo_ref[0:half, :]) + _mm(yr_ref[...], wo_ref[half:, :])
    gate = jax.nn.sigmoid(_mm(x1.astype(_BF16), wpg_ref[...]))
    x2 = x1 + gate * _mm(p_ref[...].astype(_BF16), wple_ref[...])
    if len(out_refs) == 2:
        out_refs[0][...] = x2
    hn_ref = out_refs[-1]
    hn_ref[...] = _rmsnorm(x2, g_ref[...]).astype(hn_ref.dtype)


def _merge_call(layer, x, yc, yr, p, w_out, w_pg, w_ple, g_next, rows, last):
    n, d = x.shape
    half = yc.shape[-1]
    pd = p.shape[-1]
    const = dict(pipeline_mode=pl.Buffered(1))
    in_specs = [
        pl.BlockSpec((rows, d), lambda r: (r, 0)),
        pl.BlockSpec((rows, half), lambda r: (r, 0)),
        pl.BlockSpec((rows, half), lambda r: (r, 0)),
        pl.BlockSpec((None, rows, pd), lambda r: (layer, r, 0)),
        pl.BlockSpec((None, d, d), lambda r: (layer, 0, 0), **const),
        pl.BlockSpec((None, d, d), lambda r: (layer, 0, 0), **const),
        pl.BlockSpec((None, pd, d), lambda r: (layer, 0, 0), **const),
        pl.BlockSpec((1, d), lambda r: (0, 0)),
    ]
    row_spec = pl.BlockSpec((rows, d), lambda r: (r, 0))
    if last:
        out_specs = [row_spec]
        out_shape = [jax.ShapeDtypeStruct((n, d), _F32)]
    else:
        out_specs = [row_spec, row_spec]
        out_shape = [jax.ShapeDtypeStruct((n, d), _F32), jax.ShapeDtypeStruct((n, d), _BF16)]
    return pl.pallas_call(
        _merge_kernel,
        grid=(n // rows,),
        in_specs=in_specs,
        out_specs=out_specs,
        out_shape=out_shape,
        compiler_params=pltpu.CompilerParams(
            dimension_semantics=("arbitrary",), vmem_limit_bytes=_V7X_VMEM_LIMIT_BYTES),
        name="merge_last" if last else "merge",
    )(x, yc, yr, p, w_out, w_pg, w_ple, g_next)


def _rope_tables(pos, head_dim):
    half = head_dim // 2
    inv = jnp.power(_ROPE_BASE, -jnp.arange(half, dtype=_F32) / half)
    ang = pos.astype(_F32)[:, None] * inv[None, :]
    return jnp.cos(ang), jnp.sin(ang)


def _retention_tables(head_dim):
    c = _RET_CHUNK
    log_gamma = jnp.log1p(-jnp.exp2(-5.0 - jnp.arange(_RET_HEADS, dtype=_F32)))
    idx = jnp.arange(c, dtype=_F32)
    diff = idx[:, None] - idx[None, :]
    decay = jnp.where(diff >= 0.0,
                      jnp.exp(log_gamma[:, None, None] * jnp.maximum(diff, 0.0)[None]), 0.0)
    q_dec = jnp.exp(log_gamma[:, None] * (idx[None, :] + 1.0))
    k_dec = jnp.exp(log_gamma[:, None] * (c - 1.0 - idx[None, :]))
    qdec = jnp.broadcast_to(q_dec[:, :, None], (_RET_HEADS, c, head_dim))
    kdec = jnp.broadcast_to(k_dec[:, :, None], (_RET_HEADS, c, head_dim))
    gc = jnp.broadcast_to(jnp.exp(log_gamma * c)[:, None, None], (_RET_HEADS, 1, head_dim))
    gamma1 = jnp.broadcast_to(jnp.exp(log_gamma * 1.0)[:, None, None], (_RET_HEADS, 1, head_dim))
    return decay, qdec, kdec, gc, gamma1


def kernel(x_prompt, x_sample, state_conv, state_ret, p_prompt, p_sample,
           norm_g, w_in, conv_w, gn_g, w_out, w_pg, w_ple, final_g):
    batch, seq, d = x_prompt.shape
    dec_batch, dec_seq, _ = x_sample.shape
    depth = norm_g.shape[0]
    width = conv_w.shape[-1]
    head_dim = width // _RET_HEADS
    past_len = 16384
    assert dec_seq == 1 and state_conv.shape[2] == _CONV_K - 1
    assert seq % _RET_CHUNK == 0

    tm = 512
    rows = 256
    bb = 8

    w_in_b = w_in.astype(_BF16)
    w_out_b = w_out.astype(_BF16)
    w_pg_b = w_pg.astype(_BF16)
    w_ple_b = w_ple.astype(_BF16)

    cos_p, sin_p = _rope_tables(jnp.arange(seq, dtype=jnp.int32), head_dim)
    cos_s, sin_s = _rope_tables(past_len + jnp.arange(dec_seq, dtype=jnp.int32), head_dim)
    decay, qdec, kdec, gc, gamma1 = _retention_tables(head_dim)

    norm_g3 = norm_g.reshape(depth, 1, d)
    gn_g3 = gn_g.reshape(depth, 1, width)
    final_g2 = final_g.reshape(1, d)
    p_prompt2 = p_prompt.reshape(depth, batch * seq, -1)
    p_sample2 = p_sample.reshape(depth, dec_batch * dec_seq, -1)
    state_conv_t = state_conv.transpose(0, 2, 1, 3)

    xp = x_prompt.reshape(batch * seq, d)
    xs = x_sample.reshape(dec_batch * dec_seq, d)
    hp = _norm_call(xp, norm_g3[0], rows)
    hs = _norm_call(xs, norm_g3[0], dec_batch)

    conv_p, ret_p, conv_s = [], [], []
    nret_s = None
    for i in range(depth):
        last = i == depth - 1
        g_next = final_g2 if last else norm_g3[i + 1]

        ycp, yrp, nconv_p, nret_p = _prompt_mixer_call(
            i, hp.reshape(batch, seq, d), w_in_b, conv_w, gn_g3,
            (cos_p, sin_p, decay, qdec, kdec, gc), tm)
        conv_p.append(nconv_p)
        ret_p.append(nret_p)
        outs = _merge_call(i, xp, ycp.reshape(batch * seq, width), yrp.reshape(batch * seq, width),
                           p_prompt2, w_out_b, w_pg_b, w_ple_b, g_next, rows, last)
        if last:
            y_prompt = outs[0]
        else:
            xp, hp = outs

        zs = _sample_proj_call(i, hs, w_in_b, 1024)
        ycs, yrs, nconv_s, nret_s = _sample_mixer_call(
            i, zs, state_conv_t, state_ret, conv_w, gn_g3, cos_s, sin_s, gamma1, nret_s, bb)
        conv_s.append(nconv_s)
        outs = _merge_call(i, xs, ycs, yrs, p_sample2, w_out_b, w_pg_b, w_ple_b, g_next,
                           dec_batch, last)
        if last:
            y_sample = outs[0]
        else:
            xs, hs = outs

    return (y_prompt.reshape(batch, seq, d),
            y_sample.reshape(dec_batch, dec_seq, d),
            jnp.stack(conv_p), jnp.stack(ret_p),
            jnp.stack(conv_s).transpose(0, 2, 1, 3), nret_s)
```

```python
import jax
import jax.numpy as jnp
from jax import lax
from jax.experimental import pallas as pl
from jax.experimental.pallas import tpu as pltpu

_ROPE_BASE = 10000.0
_NORM_EPS = 1e-6
_GN_EPS = 1e-6
_CONV_K = 3
_RET_HEADS = 4
_RET_CHUNK = 128
_PAST_LEN = 16384
_IN_SEGMENTS = 8

_V7X_SUBLANES = 8
_V7X_VMEM_LIMIT_BYTES = 56 * 1024 * 1024

_BF16 = jnp.bfloat16
_F32 = jnp.float32


def _mm(a, b):
    return jnp.dot(a, b, preferred_element_type=_F32)


def _mm_nt(a, b):
    return lax.dot_general(a, b, (((1,), (1,)), ((), ())), preferred_element_type=_F32)


def _rmsnorm(x, g):
    return x * lax.rsqrt(jnp.mean(x * x, axis=-1, keepdims=True) + _NORM_EPS) * g


def _rope(x, cos, sin):
    half = x.shape[-1] // 2
    x1, x2 = x[:, :half], x[:, half:]
    return jnp.concatenate([x1 * cos - x2 * sin, x1 * sin + x2 * cos], axis=-1)


def _group_norm(o):
    mu = jnp.mean(o, axis=-1, keepdims=True)
    d = o - mu
    var = jnp.mean(d * d, axis=-1, keepdims=True)
    return d * lax.rsqrt(var + _GN_EPS)


def _norm_kernel(x_ref, g_ref, h_ref):
    h_ref[...] = _rmsnorm(x_ref[...], g_ref[...]).astype(h_ref.dtype)


def _norm_call(x, g, rows):
    n, d = x.shape
    return pl.pallas_call(
        _norm_kernel,
        grid=(n // rows,),
        in_specs=[pl.BlockSpec((rows, d), lambda r: (r, 0)),
                  pl.BlockSpec((1, d), lambda r: (0, 0))],
        out_specs=pl.BlockSpec((rows, d), lambda r: (r, 0)),
        out_shape=jax.ShapeDtypeStruct((n, d), _BF16),
        compiler_params=pltpu.CompilerParams(
            dimension_semantics=("arbitrary",), vmem_limit_bytes=_V7X_VMEM_LIMIT_BYTES),
        name="rmsnorm_rows",
    )(x, g)


def _mixer_kernel(hp_ref, hs_ref, wb_ref, wc_ref, wx_ref, wg_ref, wq_ref, wk_ref, wv_ref, wr_ref,
                  convw_ref, gng_ref, cos_ref, sin_ref, decay_ref, qdec_ref, kdec_ref, gc_ref,
                  cos_s_ref, sin_s_ref, gamma_ref, buf_ref, s_ref, *rest):
    (yc_ref, yr_ref, nconv_ref, nret_ref, ycs_ref, yrs_ref, nconv_s_ref, nret_s_ref,
     tail_ref, qt_ref, kt_ref, vs_ref, zrs_ref, os_ref, ostep_ref) = rest[-15:]
    b = pl.program_id(1)
    t = pl.program_id(2)
    nt = pl.num_programs(2)
    step = b * nt + t
    tm = hp_ref.shape[1]
    nsteps, gw, nb = qt_ref.shape
    scale = gw ** -0.5
    cw = convw_ref[...]
    gng = gng_ref[...]

    @pl.when(step == 0)
    def _():
        hs = hs_ref[...]
        u = _mm(hs, wc_ref[...]) * _mm(hs, wx_ref[...])
        conv = cw[0:1] * buf_ref[0] + cw[1:2] * buf_ref[1] + cw[2:3] * u
        ycs_ref[...] = (_mm(hs, wb_ref[...]) * conv
                        * jax.nn.silu(_mm(hs, wg_ref[...]))).astype(ycs_ref.dtype)
        nconv_s_ref[0] = buf_ref[1]
        nconv_s_ref[1] = u
        cos = cos_s_ref[...]
        sin = sin_s_ref[...]
        qr = _rope(_mm(hs, wq_ref[...]), cos, sin)
        kr = _rope(_mm(hs, wk_ref[...]), cos, sin) * scale
        vs_ref[...] = _mm(hs, wv_ref[...])
        zrs_ref[...] = _mm(hs, wr_ref[...])
        for st in range(nsteps):
            qt_ref[st] = qr[st * nb:(st + 1) * nb].T
            kt_ref[st] = kr[st * nb:(st + 1) * nb].T

    @pl.when(t == 0)
    def _():
        tail_ref[...] = jnp.zeros_like(tail_ref)
        nret_ref[...] = jnp.zeros_like(nret_ref)


    row0 = pl.multiple_of(step * nb, nb)
    q_t = qt_ref[step]
    k_t = kt_ref[step]
    v_rows = vs_ref[pl.ds(row0, nb), :]
    gamma = gamma_ref[0]
    for i in range(nb):
        s_new = gamma * s_ref[i] + k_t[:, i:i + 1] * v_rows[i:i + 1, :]
        nret_s_ref[i] = s_new
        ostep_ref[i:i + 1, :] = jnp.sum(q_t[:, i:i + 1] * s_new, axis=0, keepdims=True)
    os_ref[pl.ds(row0, nb), :] = ostep_ref[...]

    h = hp_ref[0]
    u = _mm(h, wc_ref[...]) * _mm(h, wx_ref[...])
    prev = tail_ref[...]
    row = lax.broadcasted_iota(jnp.int32, u.shape, 0)
    p1 = prev[_V7X_SUBLANES - 1:_V7X_SUBLANES]
    p2 = prev[_V7X_SUBLANES - 2:_V7X_SUBLANES - 1]
    s1 = jnp.where(row == 0, p1, pltpu.roll(u, 1, 0))
    s2 = jnp.where(row == 0, p2, jnp.where(row == 1, p1, pltpu.roll(u, 2, 0)))
    conv = cw[0:1] * s2 + cw[1:2] * s1 + cw[2:3] * u
    yc_ref[0] = (_mm(h, wb_ref[...]) * conv * jax.nn.silu(_mm(h, wg_ref[...]))).astype(yc_ref.dtype)
    tail_ref[...] = u[tm - _V7X_SUBLANES:tm]

    zq = _mm(h, wq_ref[...])
    zk = _mm(h, wk_ref[...])
    zv = _mm(h, wv_ref[...])
    zr = _mm(h, wr_ref[...])
    decay = decay_ref[0]
    qdec = qdec_ref[0]
    kdec = kdec_ref[0]
    gc = gc_ref[0]
    s_state = nret_ref[0, 0]
    c = _RET_CHUNK
    for j in range(tm // c):
        sl = slice(j * c, (j + 1) * c)
        cos = cos_ref[sl, :]
        sin = sin_ref[sl, :]
        qr = _rope(zq[sl], cos, sin)
        kr = _rope(zk[sl], cos, sin) * scale
        v = zv[sl].astype(_BF16)
        qb = qr.astype(_BF16)
        scores = _mm_nt(qb, kr.astype(_BF16)) * decay
        intra = _mm(scores.astype(_BF16), v)
        inter = _mm(qb, s_state.astype(_BF16)) * qdec
        s_state = gc * s_state + _mm((kr * kdec).T.astype(_BF16), v)
        on = _group_norm(intra + inter)
        yr_ref[0, sl, :] = (on * gng * jax.nn.silu(zr[sl])).astype(yr_ref.dtype)
    nret_ref[0, 0] = s_state

    @pl.when(t == nt - 1)
    def _():
        nconv_ref[0] = tail_ref[_V7X_SUBLANES - (_CONV_K - 1):_V7X_SUBLANES, :]

    @pl.when(step == nsteps - 1)
    def _():
        on = _group_norm(os_ref[...])
        yrs_ref[...] = (on * gng * jax.nn.silu(zrs_ref[...])).astype(yrs_ref.dtype)


def _mixer_call(layer, hp, hs, w_in, conv_w, gn_g, tabs, state_conv_t, state_ret, nret_prev, tm):
    b, l, d = hp.shape
    n = hs.shape[0]
    depth = state_ret.shape[0]
    width = conv_w.shape[-1]
    gw = width // _RET_HEADS
    nt = l // tm
    nsteps = b * nt
    nb = n // nsteps
    assert nb * nsteps == n and nb % _V7X_SUBLANES == 0
    cos, sin, decay, qdec, kdec, gc, cos_s, sin_s, gamma1 = tabs

    def wspec(seg):
        return pl.BlockSpec((None, d, gw), lambda g, bi, t: (layer, 0, seg * _RET_HEADS + g),
                            pipeline_mode=pl.Buffered(1))

    def head_tab(rows, cols):
        return pl.BlockSpec((1, rows, cols), lambda g, bi, t: (g, 0, 0))

    state_spec = pl.BlockSpec((None, nb, None, gw, gw),
                              lambda g, bi, t: (layer, bi * nt + t, g, 0, 0))
    in_specs = [pl.BlockSpec((1, tm, d), lambda g, bi, t: (bi, t, 0)),
                pl.BlockSpec((n, d), lambda g, bi, t: (0, 0))]
    in_specs += [wspec(s) for s in range(_IN_SEGMENTS)]
    in_specs += [
        pl.BlockSpec((None, _CONV_K, gw), lambda g, bi, t: (layer, 0, g)),
        pl.BlockSpec((None, 1, gw), lambda g, bi, t: (layer, 0, g)),
        pl.BlockSpec((tm, gw // 2), lambda g, bi, t: (t, 0)),
        pl.BlockSpec((tm, gw // 2), lambda g, bi, t: (t, 0)),
        head_tab(_RET_CHUNK, _RET_CHUNK),
        head_tab(_RET_CHUNK, gw),
        head_tab(_RET_CHUNK, gw),
        head_tab(1, gw),
        pl.BlockSpec((1, gw // 2), lambda g, bi, t: (0, 0)),
        pl.BlockSpec((1, gw // 2), lambda g, bi, t: (0, 0)),
        head_tab(1, gw),
        pl.BlockSpec((None, _CONV_K - 1, n, gw), lambda g, bi, t: (layer, 0, 0, g)),
        state_spec,
    ]
    args = [hp, hs] + [w_in] * _IN_SEGMENTS + [conv_w, gn_g, cos, sin, decay, qdec, kdec, gc,
                                                cos_s, sin_s, gamma1, state_conv_t, state_ret]
    aliases = {}
    if nret_prev is not None:
        in_specs.append(pl.BlockSpec(memory_space=pl.ANY))
        args.append(nret_prev)
        aliases = {len(args) - 1: 7}
    out_specs = [
        pl.BlockSpec((1, tm, gw), lambda g, bi, t: (bi, t, g)),
        pl.BlockSpec((1, tm, gw), lambda g, bi, t: (bi, t, g)),
        pl.BlockSpec((1, _CONV_K - 1, gw), lambda g, bi, t: (bi, 0, g)),
        pl.BlockSpec((1, 1, gw, gw), lambda g, bi, t: (bi, g, 0, 0)),
        pl.BlockSpec((n, gw), lambda g, bi, t: (0, g)),
        pl.BlockSpec((n, gw), lambda g, bi, t: (0, g)),
        pl.BlockSpec((_CONV_K - 1, n, gw), lambda g, bi, t: (0, 0, g)),
        state_spec,
    ]
    out_shape = [
        jax.ShapeDtypeStruct((b, l, width), _BF16),
        jax.ShapeDtypeStruct((b, l, width), _BF16),
        jax.ShapeDtypeStruct((b, _CONV_K - 1, width), _F32),
        jax.ShapeDtypeStruct((b, _RET_HEADS, gw, gw), _F32),
        jax.ShapeDtypeStruct((n, width), _BF16),
        jax.ShapeDtypeStruct((n, width), _BF16),
        jax.ShapeDtypeStruct((_CONV_K - 1, n, width), _F32),
        jax.ShapeDtypeStruct((depth, n, _RET_HEADS, gw, gw), _F32),
    ]
    scratch_shapes = [
        pltpu.VMEM((_V7X_SUBLANES, gw), _F32),
        pltpu.VMEM((nsteps, gw, nb), _F32),
        pltpu.VMEM((nsteps, gw, nb), _F32),
        pltpu.VMEM((n, gw), _F32),
        pltpu.VMEM((n, gw), _F32),
        pltpu.VMEM((n, gw), _F32),
        pltpu.VMEM((nb, gw), _F32),
    ]
    return pl.pallas_call(
        _mixer_kernel,
        grid=(_RET_HEADS, b, nt),
        in_specs=in_specs,
        out_specs=out_specs,
        out_shape=out_shape,
        scratch_shapes=scratch_shapes,
        input_output_aliases=aliases,
        compiler_params=pltpu.CompilerParams(
            dimension_semantics=("arbitrary", "arbitrary", "arbitrary"),
            vmem_limit_bytes=_V7X_VMEM_LIMIT_BYTES),
        name="mixer",
    )(*args)


def _merge_rows(x_ref, yc_ref, yr_ref, p_ref, wo_ref, wpg_ref, wple_ref, g_ref, out_refs):
    half = yc_ref.shape[-1]
    x1 = x_ref[...] + _mm(yc_ref[...], wo_ref[0:half, :]) + _mm(yr_ref[...], wo_ref[half:, :])
    gate = jax.nn.sigmoid(_mm(x1.astype(_BF16), wpg_ref[...]))
    x2 = x1 + gate * _mm(p_ref[...].astype(_BF16), wple_ref[...])
    if len(out_refs) == 2:
        out_refs[0][...] = x2
    hn_ref = out_refs[-1]
    hn_ref[...] = _rmsnorm(x2, g_ref[...]).astype(hn_ref.dtype)


def _merge_kernel(xp_ref, ycp_ref, yrp_ref, pp_ref, xs_ref, ycs_ref, yrs_ref, ps_ref,
                  wo_ref, wpg_ref, wple_ref, g_ref, *out_refs):
    r = pl.program_id(0)
    n_prompt_steps = pl.num_programs(0) - 1
    k = len(out_refs) // 2

    @pl.when(r < n_prompt_steps)
    def _():
        _merge_rows(xp_ref, ycp_ref, yrp_ref, pp_ref, wo_ref, wpg_ref, wple_ref, g_ref, out_refs[:k])

    @pl.when(r == n_prompt_steps)
    def _():
        _merge_rows(xs_ref, ycs_ref, yrs_ref, ps_ref, wo_ref, wpg_ref, wple_ref, g_ref, out_refs[k:])


def _merge_call(layer, xp, ycp, yrp, pp, xs, ycs, yrs, ps, w_out, w_pg, w_ple, g_next, rows, last):
    n, d = xp.shape
    ns = xs.shape[0]
    half = ycp.shape[-1]
    pd = pp.shape[-1]
    steps = n // rows
    const = dict(pipeline_mode=pl.Buffered(1))

    def prow(r):
        return jnp.minimum(r, steps - 1)

    in_specs = [
        pl.BlockSpec((rows, d), lambda r: (prow(r), 0)),
        pl.BlockSpec((rows, half), lambda r: (prow(r), 0)),
        pl.BlockSpec((rows, half), lambda r: (prow(r), 0)),
        pl.BlockSpec((None, rows, pd), lambda r: (layer, prow(r), 0)),
        pl.BlockSpec((ns, d), lambda r: (0, 0)),
        pl.BlockSpec((ns, half), lambda r: (0, 0)),
        pl.BlockSpec((ns, half), lambda r: (0, 0)),
        pl.BlockSpec((None, ns, pd), lambda r: (layer, 0, 0)),
        pl.BlockSpec((None, d, d), lambda r: (layer, 0, 0), **const),
        pl.BlockSpec((None, d, d), lambda r: (layer, 0, 0), **const),
        pl.BlockSpec((None, pd, d), lambda r: (layer, 0, 0), **const),
        pl.BlockSpec((1, d), lambda r: (0, 0)),
    ]
    p_spec = pl.BlockSpec((rows, d), lambda r: (prow(r), 0))
    s_spec = pl.BlockSpec((ns, d), lambda r: (0, 0))
    if last:
        out_specs = [p_spec, s_spec]
        out_shape = [jax.ShapeDtypeStruct((n, d), _F32), jax.ShapeDtypeStruct((ns, d), _F32)]
    else:
        out_specs = [p_spec, p_spec, s_spec, s_spec]
        out_shape = [jax.ShapeDtypeStruct((n, d), _F32), jax.ShapeDtypeStruct((n, d), _BF16),
                     jax.ShapeDtypeStruct((ns, d), _F32), jax.ShapeDtypeStruct((ns, d), _BF16)]
    return pl.pallas_call(
        _merge_kernel,
        grid=(steps + 1,),
        in_specs=in_specs,
        out_specs=out_specs,
        out_shape=out_shape,
        compiler_params=pltpu.CompilerParams(
            dimension_semantics=("arbitrary",), vmem_limit_bytes=_V7X_VMEM_LIMIT_BYTES),
        name="merge_last" if last else "merge",
    )(xp, ycp, yrp, pp, xs, ycs, yrs, ps, w_out, w_pg, w_ple, g_next)


def _rope_tables(pos, head_dim):
    half = head_dim // 2
    inv = jnp.power(_ROPE_BASE, -jnp.arange(half, dtype=_F32) / half)
    ang = pos.astype(_F32)[:, None] * inv[None, :]
    return jnp.cos(ang), jnp.sin(ang)


def _retention_tables(head_dim):
    c = _RET_CHUNK
    log_gamma = jnp.log1p(-jnp.exp2(-5.0 - jnp.arange(_RET_HEADS, dtype=_F32)))
    idx = jnp.arange(c, dtype=_F32)
    diff = idx[:, None] - idx[None, :]
    decay = jnp.where(diff >= 0.0,
                      jnp.exp(log_gamma[:, None, None] * jnp.maximum(diff, 0.0)[None]), 0.0)
    q_dec = jnp.exp(log_gamma[:, None] * (idx[None, :] + 1.0))
    k_dec = jnp.exp(log_gamma[:, None] * (c - 1.0 - idx[None, :]))
    qdec = jnp.broadcast_to(q_dec[:, :, None], (_RET_HEADS, c, head_dim))
    kdec = jnp.broadcast_to(k_dec[:, :, None], (_RET_HEADS, c, head_dim))
    gc = jnp.broadcast_to(jnp.exp(log_gamma * c)[:, None, None], (_RET_HEADS, 1, head_dim))
    gamma1 = jnp.broadcast_to(jnp.exp(log_gamma * 1.0)[:, None, None], (_RET_HEADS, 1, head_dim))
    return decay, qdec, kdec, gc, gamma1


def kernel(x_prompt, x_sample, state_conv, state_ret, p_prompt, p_sample,
           norm_g, w_in, conv_w, gn_g, w_out, w_pg, w_ple, final_g):
    batch, seq, d = x_prompt.shape
    dec_batch, dec_seq, _ = x_sample.shape
    depth = norm_g.shape[0]
    width = conv_w.shape[-1]
    head_dim = width // _RET_HEADS
    assert dec_seq == 1 and state_conv.shape[2] == _CONV_K - 1
    assert seq % _RET_CHUNK == 0

    tm = 1024
    rows = 256

    w_in_b = w_in.astype(_BF16)
    w_out_b = w_out.astype(_BF16)
    w_pg_b = w_pg.astype(_BF16)
    w_ple_b = w_ple.astype(_BF16)

    cos_p, sin_p = _rope_tables(jnp.arange(seq, dtype=jnp.int32), head_dim)
    cos_s, sin_s = _rope_tables(_PAST_LEN + jnp.arange(dec_seq, dtype=jnp.int32), head_dim)
    decay, qdec, kdec, gc, gamma1 = _retention_tables(head_dim)
    tabs = (cos_p, sin_p, decay, qdec, kdec, gc, cos_s, sin_s, gamma1)

    norm_g3 = norm_g.reshape(depth, 1, d)
    gn_g3 = gn_g.reshape(depth, 1, width)
    final_g2 = final_g.reshape(1, d)
    p_prompt2 = p_prompt.reshape(depth, batch * seq, -1)
    p_sample2 = p_sample.reshape(depth, dec_batch * dec_seq, -1)
    state_conv_t = state_conv.transpose(0, 2, 1, 3)

    xp = x_prompt.reshape(batch * seq, d)
    xs = x_sample.reshape(dec_batch * dec_seq, d)
    hp = _norm_call(xp, norm_g3[0], rows)
    hs = _norm_call(xs, norm_g3[0], dec_batch)

    conv_p, ret_p, conv_s = [], [], []
    nret_s = None
    for i in range(depth):
        last = i == depth - 1
        g_next = final_g2 if last else norm_g3[i + 1]
        ycp, yrp, nconv_p, nret_p, ycs, yrs, nconv_s, nret_s = _mixer_call(
            i, hp.reshape(batch, seq, d), hs, w_in_b, conv_w, gn_g3, tabs,
            state_conv_t, state_ret, nret_s, tm)
        conv_p.append(nconv_p)
        ret_p.append(nret_p)
        conv_s.append(nconv_s)
        outs = _merge_call(i, xp, ycp.reshape(batch * seq, width), yrp.reshape(batch * seq, width),
                           p_prompt2, xs, ycs, yrs, p_sample2,
                           w_out_b, w_pg_b, w_ple_b, g_next, rows, last)
        if last:
            y_prompt, y_sample = outs
        else:
            xp, hp, xs, hs = outs

    return (y_prompt.reshape(batch, seq, d),
            y_sample.reshape(dec_batch, dec_seq, d),
            jnp.stack(conv_p), jnp.stack(ret_p),
            jnp.stack(conv_s).transpose(0, 2, 1, 3), nret_s)
```

```python
import functools

import jax
import jax.numpy as jnp
from jax import lax
from jax.experimental import pallas as pl
from jax.experimental.pallas import tpu as pltpu

_ROPE_BASE = 10000.0
_NORM_EPS = 1e-6
_GN_EPS = 1e-6
_CONV_K = 3
_RET_HEADS = 4
_RET_CHUNK = 128
_PAST_LEN = 16384
_IN_SEGMENTS = 8

_V7X_SUBLANES = 8
_V7X_VMEM_LIMIT_BYTES = 56 * 1024 * 1024

_BF16 = jnp.bfloat16
_F32 = jnp.float32


def _mm(a, b):
    return jnp.dot(a, b, preferred_element_type=_F32)


def _mm_nt(a, b):
    return lax.dot_general(a, b, (((1,), (1,)), ((), ())), preferred_element_type=_F32)


def _rmsnorm(x, g):
    return x * lax.rsqrt(jnp.mean(x * x, axis=-1, keepdims=True) + _NORM_EPS) * g


def _rope(x, cos, sin):
    half = x.shape[-1] // 2
    x1, x2 = x[:, :half], x[:, half:]
    return jnp.concatenate([x1 * cos - x2 * sin, x1 * sin + x2 * cos], axis=-1)


def _group_norm(o):
    mu = jnp.mean(o, axis=-1, keepdims=True)
    d = o - mu
    var = jnp.mean(d * d, axis=-1, keepdims=True)
    return d * lax.rsqrt(var + _GN_EPS)


def _norm_kernel(x_ref, g_ref, h_ref):
    h_ref[...] = _rmsnorm(x_ref[...], g_ref[...]).astype(h_ref.dtype)


def _norm_call(x, g, rows):
    n, d = x.shape
    return pl.pallas_call(
        _norm_kernel,
        grid=(n // rows,),
        in_specs=[pl.BlockSpec((rows, d), lambda r: (r, 0)),
                  pl.BlockSpec((1, d), lambda r: (0, 0))],
        out_specs=pl.BlockSpec((rows, d), lambda r: (r, 0)),
        out_shape=jax.ShapeDtypeStruct((n, d), _BF16),
        compiler_params=pltpu.CompilerParams(
            dimension_semantics=("arbitrary",), vmem_limit_bytes=_V7X_VMEM_LIMIT_BYTES),
        name="rmsnorm_rows",
    )(x, g)


def _mixer_kernel(hp_ref, hs_ref, wb_ref, wc_ref, wx_ref, wg_ref, wq_ref, wk_ref, wv_ref, wr_ref,
                  convw_ref, gng_ref, cos_ref, sin_ref, decay_ref, qdec_ref, kdec_ref, gc_ref,
                  cos_s_ref, sin_s_ref, gamma_ref, buf_ref, s_ref, *rest):
    (yc_ref, yr_ref, nconv_ref, nret_ref, ycs_ref, yrs_ref, nconv_s_ref, nret_s_ref,
     tail_ref, qt_ref, kt_ref, vs_ref, zrs_ref, os_ref, ostep_ref) = rest[-15:]
    b = pl.program_id(1)
    t = pl.program_id(2)
    nt = pl.num_programs(2)
    step = b * nt + t
    tm = hp_ref.shape[1]
    nsteps, gw, nb = qt_ref.shape
    scale = gw ** -0.5
    cw = convw_ref[...]
    gng = gng_ref[...]

    @pl.when(step == 0)
    def _():
        hs = hs_ref[...]
        u = _mm(hs, wc_ref[...]) * _mm(hs, wx_ref[...])
        conv = cw[0:1] * buf_ref[0] + cw[1:2] * buf_ref[1] + cw[2:3] * u
        ycs_ref[...] = (_mm(hs, wb_ref[...]) * conv
                        * jax.nn.silu(_mm(hs, wg_ref[...]))).astype(ycs_ref.dtype)
        nconv_s_ref[0] = buf_ref[1]
        nconv_s_ref[1] = u
        cos = cos_s_ref[...]
        sin = sin_s_ref[...]
        qr = _rope(_mm(hs, wq_ref[...]), cos, sin)
        kr = _rope(_mm(hs, wk_ref[...]), cos, sin) * scale
        vs_ref[...] = _mm(hs, wv_ref[...])
        zrs_ref[...] = _mm(hs, wr_ref[...])
        for st in range(nsteps):
            qt_ref[st] = qr[st * nb:(st + 1) * nb].T
            kt_ref[st] = kr[st * nb:(st + 1) * nb].T

    @pl.when(t == 0)
    def _():
        tail_ref[...] = jnp.zeros_like(tail_ref)
        nret_ref[...] = jnp.zeros_like(nret_ref)


    row0 = pl.multiple_of(step * nb, nb)
    q_t = qt_ref[step]
    k_t = kt_ref[step]
    v_rows = vs_ref[pl.ds(row0, nb), :]
    gamma = gamma_ref[0]
    for i in range(nb):
        s_new = gamma * s_ref[i] + k_t[:, i:i + 1] * v_rows[i:i + 1, :]
        nret_s_ref[i] = s_new
        ostep_ref[i:i + 1, :] = jnp.sum(q_t[:, i:i + 1] * s_new, axis=0, keepdims=True)
    os_ref[pl.ds(row0, nb), :] = ostep_ref[...]

    h = hp_ref[0]
    zq = _mm(h, wq_ref[...])
    zk = _mm(h, wk_ref[...])
    zv = _mm(h, wv_ref[...])
    zr = _mm(h, wr_ref[...])
    decay = decay_ref[0]
    qdec = qdec_ref[0]
    kdec = kdec_ref[0]
    gc = gc_ref[0]
    c = _RET_CHUNK
    n_chunks = tm // c

    qbs, vs, scores, incs = [], [], [], []
    for j in range(n_chunks):
        sl = slice(j * c, (j + 1) * c)
        cos = cos_ref[sl, :]
        sin = sin_ref[sl, :]
        qr = _rope(zq[sl], cos, sin)
        kr = _rope(zk[sl], cos, sin) * scale
        v = zv[sl].astype(_BF16)
        qb = qr.astype(_BF16)
        qbs.append(qb)
        vs.append(v)
        scores.append(_mm_nt(qb, kr.astype(_BF16)))
        incs.append(_mm((kr * kdec).T.astype(_BF16), v))

    u = _mm(h, wc_ref[...]) * _mm(h, wx_ref[...])
    prev = tail_ref[...]
    row = lax.broadcasted_iota(jnp.int32, u.shape, 0)
    p1 = prev[_V7X_SUBLANES - 1:_V7X_SUBLANES]
    p2 = prev[_V7X_SUBLANES - 2:_V7X_SUBLANES - 1]
    s1 = jnp.where(row == 0, p1, pltpu.roll(u, 1, 0))
    s2 = jnp.where(row == 0, p2, jnp.where(row == 1, p1, pltpu.roll(u, 2, 0)))
    conv = cw[0:1] * s2 + cw[1:2] * s1 + cw[2:3] * u
    tail_ref[...] = u[tm - _V7X_SUBLANES:tm]

    s_state = nret_ref[0, 0]
    for j in range(n_chunks):
        sl = slice(j * c, (j + 1) * c)
        intra = _mm((scores[j] * decay).astype(_BF16), vs[j])
        inter = _mm(qbs[j], s_state.astype(_BF16)) * qdec
        s_state = gc * s_state + incs[j]
        on = _group_norm(intra + inter)
        yr_ref[0, sl, :] = (on * gng * jax.nn.silu(zr[sl])).astype(yr_ref.dtype)
    nret_ref[0, 0] = s_state

    yc_ref[0] = (_mm(h, wb_ref[...]) * conv * jax.nn.silu(_mm(h, wg_ref[...]))).astype(yc_ref.dtype)

    @pl.when(t == nt - 1)
    def _():
        nconv_ref[0] = tail_ref[_V7X_SUBLANES - (_CONV_K - 1):_V7X_SUBLANES, :]

    @pl.when(step == nsteps - 1)
    def _():
        on = _group_norm(os_ref[...])
        yrs_ref[...] = (on * gng * jax.nn.silu(zrs_ref[...])).astype(yrs_ref.dtype)


def _mixer_call(layer, hp, hs, w_in, conv_w, gn_g, tabs, state_conv_t, state_ret, nret_prev, tm):
    b, l, d = hp.shape
    n = hs.shape[0]
    depth = state_ret.shape[0]
    width = conv_w.shape[-1]
    gw = width // _RET_HEADS
    nt = l // tm
    nsteps = b * nt
    nb = n // nsteps
    assert nb * nsteps == n and nb % _V7X_SUBLANES == 0
    cos, sin, decay, qdec, kdec, gc, cos_s, sin_s, gamma1 = tabs

    def wspec(seg):
        return pl.BlockSpec((d, gw), lambda g, bi, t: (0, seg * _RET_HEADS + g),
                            pipeline_mode=pl.Buffered(1))

    def head_tab(rows, cols):
        return pl.BlockSpec((1, rows, cols), lambda g, bi, t: (g, 0, 0))

    state_spec = pl.BlockSpec((None, nb, None, gw, gw),
                              lambda g, bi, t: (layer, bi * nt + t, g, 0, 0))
    in_specs = [pl.BlockSpec((1, tm, d), lambda g, bi, t: (bi, t, 0)),
                pl.BlockSpec((n, d), lambda g, bi, t: (0, 0))]
    in_specs += [wspec(s) for s in range(_IN_SEGMENTS)]
    in_specs += [
        pl.BlockSpec((None, _CONV_K, gw), lambda g, bi, t: (layer, 0, g)),
        pl.BlockSpec((None, 1, gw), lambda g, bi, t: (layer, 0, g)),
        pl.BlockSpec((tm, gw // 2), lambda g, bi, t: (t, 0)),
        pl.BlockSpec((tm, gw // 2), lambda g, bi, t: (t, 0)),
        head_tab(_RET_CHUNK, _RET_CHUNK),
        head_tab(_RET_CHUNK, gw),
        head_tab(_RET_CHUNK, gw),
        head_tab(1, gw),
        pl.BlockSpec((1, gw // 2), lambda g, bi, t: (0, 0)),
        pl.BlockSpec((1, gw // 2), lambda g, bi, t: (0, 0)),
        head_tab(1, gw),
        pl.BlockSpec((None, _CONV_K - 1, n, gw), lambda g, bi, t: (layer, 0, 0, g)),
        state_spec,
    ]
    args = [hp, hs] + [w_in] * _IN_SEGMENTS + [conv_w, gn_g, cos, sin, decay, qdec, kdec, gc,
                                                cos_s, sin_s, gamma1, state_conv_t, state_ret]
    aliases = {}
    if nret_prev is not None:
        in_specs.append(pl.BlockSpec(memory_space=pl.ANY))
        args.append(nret_prev)
        aliases = {len(args) - 1: 7}
    out_specs = [
        pl.BlockSpec((1, tm, gw), lambda g, bi, t: (bi, t, g)),
        pl.BlockSpec((1, tm, gw), lambda g, bi, t: (bi, t, g)),
        pl.BlockSpec((1, _CONV_K - 1, gw), lambda g, bi, t: (bi, 0, g)),
        pl.BlockSpec((1, 1, gw, gw), lambda g, bi, t: (bi, g, 0, 0)),
        pl.BlockSpec((n, gw), lambda g, bi, t: (0, g)),
        pl.BlockSpec((n, gw), lambda g, bi, t: (0, g)),
        pl.BlockSpec((_CONV_K - 1, n, gw), lambda g, bi, t: (0, 0, g)),
        state_spec,
    ]
    out_shape = [
        jax.ShapeDtypeStruct((b, l, width), _BF16),
        jax.ShapeDtypeStruct((b, l, width), _BF16),
        jax.ShapeDtypeStruct((b, _CONV_K - 1, width), _F32),
        jax.ShapeDtypeStruct((b, _RET_HEADS, gw, gw), _F32),
        jax.ShapeDtypeStruct((n, width), _BF16),
        jax.ShapeDtypeStruct((n, width), _BF16),
        jax.ShapeDtypeStruct((_CONV_K - 1, n, width), _F32),
        jax.ShapeDtypeStruct((depth, n, _RET_HEADS, gw, gw), _F32),
    ]
    scratch_shapes = [
        pltpu.VMEM((_V7X_SUBLANES, gw), _F32),
        pltpu.VMEM((nsteps, gw, nb), _F32),
        pltpu.VMEM((nsteps, gw, nb), _F32),
        pltpu.VMEM((n, gw), _F32),
        pltpu.VMEM((n, gw), _F32),
        pltpu.VMEM((n, gw), _F32),
        pltpu.VMEM((nb, gw), _F32),
    ]
    return pl.pallas_call(
        _mixer_kernel,
        grid=(_RET_HEADS, b, nt),
        in_specs=in_specs,
        out_specs=out_specs,
        out_shape=out_shape,
        scratch_shapes=scratch_shapes,
        input_output_aliases=aliases,
        compiler_params=pltpu.CompilerParams(
            dimension_semantics=("arbitrary", "arbitrary", "arbitrary"),
            vmem_limit_bytes=_V7X_VMEM_LIMIT_BYTES),
        name="mixer",
    )(*args)


def _merge_rows(x_ref, yc_ref, yr_ref, p_ref, wo_ref, wpg_ref, wple_ref, g_ref, out_refs):
    half = yc_ref.shape[-1]
    x1 = x_ref[...] + _mm(yc_ref[...], wo_ref[0:half, :]) + _mm(yr_ref[...], wo_ref[half:, :])
    gate = jax.nn.sigmoid(_mm(x1.astype(_BF16), wpg_ref[...]))
    x2 = x1 + gate * _mm(p_ref[...].astype(_BF16), wple_ref[...])
    if len(out_refs) == 2:
        out_refs[0][...] = x2
    hn_ref = out_refs[-1]
    hn_ref[...] = _rmsnorm(x2, g_ref[...]).astype(hn_ref.dtype)


def _merge_kernel(xp_ref, ycp_ref, yrp_ref, pp_ref, xs_ref, ycs_ref, yrs_ref, ps_ref,
                  wo_ref, wpg_ref, wple_ref, g_ref, *refs, last):
    r = pl.program_id(0)
    n_prompt_steps = pl.num_programs(0) - 1
    if last:
        prompt_outs, sample_outs = refs[0:1], refs[1:2]
    else:
        nw_refs, refs = refs[:4], refs[4:]
        prompt_outs, sample_outs, cw_refs = refs[0:2], refs[2:4], refs[4:8]

    @pl.when(r < n_prompt_steps)
    def _():
        _merge_rows(xp_ref, ycp_ref, yrp_ref, pp_ref, wo_ref, wpg_ref, wple_ref, g_ref, prompt_outs)
        if not last:
            for src, dst in zip(nw_refs[:3], cw_refs[:3]):
                dst[...] = src[...].astype(dst.dtype)

    @pl.when(r == n_prompt_steps)
    def _():
        _merge_rows(xs_ref, ycs_ref, yrs_ref, ps_ref, wo_ref, wpg_ref, wple_ref, g_ref, sample_outs)
        if not last:
            cw_refs[3][...] = nw_refs[3][...].astype(cw_refs[3].dtype)


def _merge_call(layer, xp, ycp, yrp, pp, xs, ycs, yrs, ps, w_out_b, w_pg_b, w_ple_b, g_next,
                next_f32, rows, last):
    n, d = xp.shape
    ns = xs.shape[0]
    half = ycp.shape[-1]
    pd = pp.shape[-1]
    steps = n // rows
    const = dict(pipeline_mode=pl.Buffered(1))

    def prow(r):
        return jnp.minimum(r, steps - 1)

    in_specs = [
        pl.BlockSpec((rows, d), lambda r: (prow(r), 0)),
        pl.BlockSpec((rows, half), lambda r: (prow(r), 0)),
        pl.BlockSpec((rows, half), lambda r: (prow(r), 0)),
        pl.BlockSpec((None, rows, pd), lambda r: (layer, prow(r), 0)),
        pl.BlockSpec((ns, d), lambda r: (0, 0)),
        pl.BlockSpec((ns, half), lambda r: (0, 0)),
        pl.BlockSpec((ns, half), lambda r: (0, 0)),
        pl.BlockSpec((None, ns, pd), lambda r: (layer, 0, 0)),
        pl.BlockSpec((d, d), lambda r: (0, 0), **const),
        pl.BlockSpec((d, d), lambda r: (0, 0), **const),
        pl.BlockSpec((pd, d), lambda r: (0, 0), **const),
        pl.BlockSpec((1, d), lambda r: (0, 0)),
    ]
    args = [xp, ycp, yrp, pp, xs, ycs, yrs, ps, w_out_b, w_pg_b, w_ple_b, g_next]
    p_spec = pl.BlockSpec((rows, d), lambda r: (prow(r), 0))
    s_spec = pl.BlockSpec((ns, d), lambda r: (0, 0))
    if last:
        out_specs = [p_spec, s_spec]
        out_shape = [jax.ShapeDtypeStruct((n, d), _F32), jax.ShapeDtypeStruct((ns, d), _F32)]
    else:
        out_specs = [p_spec, p_spec, s_spec, s_spec]
        out_shape = [jax.ShapeDtypeStruct((n, d), _F32), jax.ShapeDtypeStruct((n, d), _BF16),
                     jax.ShapeDtypeStruct((ns, d), _F32), jax.ShapeDtypeStruct((ns, d), _BF16)]
        for w in next_f32[:3]:
            slab = w.shape[1] // steps
            in_specs.append(pl.BlockSpec((None, slab, w.shape[2]), lambda r: (layer + 1, prow(r), 0)))
            out_specs.append(pl.BlockSpec((slab, w.shape[2]), lambda r: (prow(r), 0)))
            out_shape.append(jax.ShapeDtypeStruct(w.shape[1:], _BF16))
        w = next_f32[3]
        in_specs.append(pl.BlockSpec((None,) + w.shape[1:], lambda r: (layer + 1, 0, 0), **const))
        out_specs.append(pl.BlockSpec(w.shape[1:], lambda r: (0, 0)))
        out_shape.append(jax.ShapeDtypeStruct(w.shape[1:], _BF16))
        args += list(next_f32)
    return pl.pallas_call(
        functools.partial(_merge_kernel, last=last),
        grid=(steps + 1,),
        in_specs=in_specs,
        out_specs=out_specs,
        out_shape=out_shape,
        compiler_params=pltpu.CompilerParams(
            dimension_semantics=("arbitrary",), vmem_limit_bytes=_V7X_VMEM_LIMIT_BYTES),
        name="merge_last" if last else "merge",
    )(*args)


def _rope_tables(pos, head_dim):
    half = head_dim // 2
    inv = jnp.power(_ROPE_BASE, -jnp.arange(half, dtype=_F32) / half)
    ang = pos.astype(_F32)[:, None] * inv[None, :]
    return jnp.cos(ang), jnp.sin(ang)


def _retention_tables(head_dim):
    c = _RET_CHUNK
    log_gamma = jnp.log1p(-jnp.exp2(-5.0 - jnp.arange(_RET_HEADS, dtype=_F32)))
    idx = jnp.arange(c, dtype=_F32)
    diff = idx[:, None] - idx[None, :]
    decay = jnp.where(diff >= 0.0,
                      jnp.exp(log_gamma[:, None, None] * jnp.maximum(diff, 0.0)[None]), 0.0)
    q_dec = jnp.exp(log_gamma[:, None] * (idx[None, :] + 1.0))
    k_dec = jnp.exp(log_gamma[:, None] * (c - 1.0 - idx[None, :]))
    qdec = jnp.broadcast_to(q_dec[:, :, None], (_RET_HEADS, c, head_dim))
    kdec = jnp.broadcast_to(k_dec[:, :, None], (_RET_HEADS, c, head_dim))
    gc = jnp.broadcast_to(jnp.exp(log_gamma * c)[:, None, None], (_RET_HEADS, 1, head_dim))
    gamma1 = jnp.broadcast_to(jnp.exp(log_gamma * 1.0)[:, None, None], (_RET_HEADS, 1, head_dim))
    return decay, qdec, kdec, gc, gamma1


def kernel(x_prompt, x_sample, state_conv, state_ret, p_prompt, p_sample,
           norm_g, w_in, conv_w, gn_g, w_out, w_pg, w_ple, final_g):
    batch, seq, d = x_prompt.shape
    dec_batch, dec_seq, _ = x_sample.shape
    depth = norm_g.shape[0]
    width = conv_w.shape[-1]
    head_dim = width // _RET_HEADS
    assert dec_seq == 1 and state_conv.shape[2] == _CONV_K - 1
    assert seq % _RET_CHUNK == 0

    tm = 1024
    rows = 256
    norm_rows = 1024

    weights_f32 = (w_in, w_out, w_pg, w_ple)
    w_in_b, w_out_b, w_pg_b, w_ple_b = (w[0].astype(_BF16) for w in weights_f32)

    cos_p, sin_p = _rope_tables(jnp.arange(seq, dtype=jnp.int32), head_dim)
    cos_s, sin_s = _rope_tables(_PAST_LEN + jnp.arange(dec_seq, dtype=jnp.int32), head_dim)
    decay, qdec, kdec, gc, gamma1 = _retention_tables(head_dim)
    tabs = (cos_p, sin_p, decay, qdec, kdec, gc, cos_s, sin_s, gamma1)

    norm_g3 = norm_g.reshape(depth, 1, d)
    gn_g3 = gn_g.reshape(depth, 1, width)
    final_g2 = final_g.reshape(1, d)
    p_prompt2 = p_prompt.reshape(depth, batch * seq, -1)
    p_sample2 = p_sample.reshape(depth, dec_batch * dec_seq, -1)
    state_conv_t = state_conv.transpose(0, 2, 1, 3)

    xp = x_prompt.reshape(batch * seq, d)
    xs = x_sample.reshape(dec_batch * dec_seq, d)
    hp = _norm_call(xp, norm_g3[0], norm_rows)
    hs = _norm_call(xs, norm_g3[0], dec_batch)

    conv_p, ret_p, conv_s = [], [], []
    nret_s = None
    for i in range(depth):
        last = i == depth - 1
        g_next = final_g2 if last else norm_g3[i + 1]
        ycp, yrp, nconv_p, nret_p, ycs, yrs, nconv_s, nret_s = _mixer_call(
            i, hp.reshape(batch, seq, d), hs, w_in_b, conv_w, gn_g3, tabs,
            state_conv_t, state_ret, nret_s, tm)
        conv_p.append(nconv_p)
        ret_p.append(nret_p)
        conv_s.append(nconv_s)
        outs = _merge_call(i, xp, ycp.reshape(batch * seq, width), yrp.reshape(batch * seq, width),
                           p_prompt2, xs, ycs, yrs, p_sample2,
                           w_out_b, w_pg_b, w_ple_b, g_next, weights_f32, rows, last)
        if last:
            y_prompt, y_sample = outs
        else:
            xp, hp, xs, hs, w_in_b, w_out_b, w_pg_b, w_ple_b = outs

    return (y_prompt.reshape(batch, seq, d),
            y_sample.reshape(dec_batch, dec_seq, d),
            jnp.stack(conv_p), jnp.stack(ret_p),
            jnp.stack(conv_s).transpose(0, 2, 1, 3), nret_s)
```

```python
import functools

import jax
import jax.numpy as jnp
from jax import lax
from jax.experimental import pallas as pl
from jax.experimental.pallas import tpu as pltpu

_ROPE_BASE = 10000.0
_NORM_EPS = 1e-6
_GN_EPS = 1e-6
_CONV_K = 3
_RET_HEADS = 4
_RET_CHUNK = 256
_PAST_LEN = 16384
_IN_SEGMENTS = 8

_V7X_SUBLANES = 8
_V7X_VMEM_LIMIT_BYTES = 62 * 1024 * 1024

_BF16 = jnp.bfloat16
_F32 = jnp.float32


def _mm(a, b):
    return jnp.dot(a, b, preferred_element_type=_F32)


def _mm_nt(a, b):
    return lax.dot_general(a, b, (((1,), (1,)), ((), ())), preferred_element_type=_F32)


_sigmoid = jax.nn.sigmoid
_silu = jax.nn.silu


def _rmsnorm(x, g):
    return x * lax.rsqrt(jnp.mean(x * x, axis=-1, keepdims=True) + _NORM_EPS) * g


def _rope(x, cos, sin):
    half = x.shape[-1] // 2
    x1, x2 = x[:, :half], x[:, half:]
    return jnp.concatenate([x1 * cos - x2 * sin, x1 * sin + x2 * cos], axis=-1)


def _group_norm(o):
    mu = jnp.mean(o, axis=-1, keepdims=True)
    d = o - mu
    var = jnp.mean(d * d, axis=-1, keepdims=True)
    return d * lax.rsqrt(var + _GN_EPS)


def _norm_kernel(x_ref, g_ref, h_ref):
    h_ref[...] = _rmsnorm(x_ref[...], g_ref[...]).astype(h_ref.dtype)


def _norm_call(x, g, rows):
    n, d = x.shape
    return pl.pallas_call(
        _norm_kernel,
        grid=(n // rows,),
        in_specs=[pl.BlockSpec((rows, d), lambda r: (r, 0)),
                  pl.BlockSpec((1, d), lambda r: (0, 0))],
        out_specs=pl.BlockSpec((rows, d), lambda r: (r, 0)),
        out_shape=jax.ShapeDtypeStruct((n, d), _BF16),
        compiler_params=pltpu.CompilerParams(
            dimension_semantics=("arbitrary",), vmem_limit_bytes=_V7X_VMEM_LIMIT_BYTES),
        name="rmsnorm_rows",
    )(x, g)


def _mixer_kernel(hp_ref, hs_ref, wb_ref, wc_ref, wx_ref, wg_ref, wq_ref, wk_ref, wv_ref, wr_ref,
                  convw_ref, gng_ref, cos_ref, sin_ref, decay_ref, qdec_ref, kdec_ref, gc_ref,
                  cos_s_ref, sin_s_ref, gamma_ref, buf_ref, s_ref, *rest):
    (yc_ref, yr_ref, nconv_ref, nret_ref, ycs_ref, yrs_ref, nconv_s_ref, nret_s_ref,
     tail_ref, qt_ref, kt_ref, vs_ref, zrs_ref, os_ref, ostep_ref) = rest[-15:]
    b = pl.program_id(1)
    t = pl.program_id(2)
    nt = pl.num_programs(2)
    step = b * nt + t
    tm = hp_ref.shape[1]
    nsteps, gw, nb = qt_ref.shape
    scale = gw ** -0.5
    cw = convw_ref[...]
    gng = gng_ref[...]

    @pl.when(step == 0)
    def _():
        hs = hs_ref[...]
        u = _mm(hs, wc_ref[...]) * _mm(hs, wx_ref[...])
        conv = cw[0:1] * buf_ref[0] + cw[1:2] * buf_ref[1] + cw[2:3] * u
        ycs_ref[...] = (_mm(hs, wb_ref[...]) * conv
                        * _silu(_mm(hs, wg_ref[...]))).astype(ycs_ref.dtype)
        nconv_s_ref[0] = buf_ref[1]
        nconv_s_ref[1] = u
        cos = cos_s_ref[...]
        sin = sin_s_ref[...]
        qr = _rope(_mm(hs, wq_ref[...]), cos, sin)
        kr = _rope(_mm(hs, wk_ref[...]), cos, sin) * scale
        vs_ref[...] = _mm(hs, wv_ref[...])
        zrs_ref[...] = _mm(hs, wr_ref[...])
        for st in range(nsteps):
            qt_ref[st] = qr[st * nb:(st + 1) * nb].T
            kt_ref[st] = kr[st * nb:(st + 1) * nb].T

    @pl.when(t == 0)
    def _():
        tail_ref[...] = jnp.zeros_like(tail_ref)
        nret_ref[...] = jnp.zeros_like(nret_ref)


    row0 = pl.multiple_of(step * nb, nb)
    q_t = qt_ref[step]
    k_t = kt_ref[step]
    v_rows = vs_ref[pl.ds(row0, nb), :]
    gamma = gamma_ref[0]
    for i in range(nb):
        s_new = gamma * s_ref[i] + k_t[:, i:i + 1] * v_rows[i:i + 1, :]
        nret_s_ref[i] = s_new
        ostep_ref[i:i + 1, :] = jnp.sum(q_t[:, i:i + 1] * s_new, axis=0, keepdims=True)
    os_ref[pl.ds(row0, nb), :] = ostep_ref[...]

    h = hp_ref[0]
    zq = _mm(h, wq_ref[...])
    zk = _mm(h, wk_ref[...])
    zv = _mm(h, wv_ref[...])
    zr = _mm(h, wr_ref[...])
    decay = decay_ref[0]
    qdec = qdec_ref[0]
    kdec = kdec_ref[0]
    gc = gc_ref[0]
    c = decay.shape[0]
    n_chunks = tm // c

    qbs, vs, scores, incs = [], [], [], []
    for j in range(n_chunks):
        sl = slice(j * c, (j + 1) * c)
        cos = cos_ref[sl, :]
        sin = sin_ref[sl, :]
        qr = _rope(zq[sl], cos, sin)
        kr = _rope(zk[sl], cos, sin) * scale
        v = zv[sl].astype(_BF16)
        qb = qr.astype(_BF16)
        qbs.append(qb)
        vs.append(v)
        scores.append(_mm_nt(qb, kr.astype(_BF16)))
        incs.append(_mm((kr * kdec).T.astype(_BF16), v))

    u = _mm(h, wc_ref[...]) * _mm(h, wx_ref[...])
    prev = tail_ref[...]
    row = lax.broadcasted_iota(jnp.int32, u.shape, 0)
    p1 = prev[_V7X_SUBLANES - 1:_V7X_SUBLANES]
    p2 = prev[_V7X_SUBLANES - 2:_V7X_SUBLANES - 1]
    s1 = jnp.where(row == 0, p1, pltpu.roll(u, 1, 0))
    s2 = jnp.where(row == 0, p2, jnp.where(row == 1, p1, pltpu.roll(u, 2, 0)))
    conv = cw[0:1] * s2 + cw[1:2] * s1 + cw[2:3] * u
    tail_ref[...] = u[tm - _V7X_SUBLANES:tm]

    s_state = nret_ref[0, 0]
    for j in range(n_chunks):
        sl = slice(j * c, (j + 1) * c)
        intra = _mm((scores[j] * decay).astype(_BF16), vs[j])
        inter = _mm(qbs[j], s_state.astype(_BF16)) * qdec
        s_state = gc * s_state + incs[j]
        on = _group_norm(intra + inter)
        yr_ref[0, sl, :] = (on * gng * _silu(zr[sl])).astype(yr_ref.dtype)
    nret_ref[0, 0] = s_state

    gated = _mm(h, wb_ref[...]) * conv
    split = tm - tm // 4
    for rs in (slice(0, split), slice(split, tm)):
        zg = _mm(h[rs], wg_ref[...])
        yc_ref[0, rs, :] = (gated[rs] * _silu(zg)).astype(yc_ref.dtype)

    @pl.when(t == nt - 1)
    def _():
        nconv_ref[0] = tail_ref[_V7X_SUBLANES - (_CONV_K - 1):_V7X_SUBLANES, :]

    @pl.when(step == nsteps - 1)
    def _():
        on = _group_norm(os_ref[...])
        yrs_ref[...] = (on * gng * _silu(zrs_ref[...])).astype(yrs_ref.dtype)


def _mixer_call(layer, hp, hs, w_in, conv_w, gn_g, tabs, state_conv_t, state_ret, nret_prev, tm):
    b, l, d = hp.shape
    n = hs.shape[0]
    depth = state_ret.shape[0]
    width = conv_w.shape[-1]
    gw = width // _RET_HEADS
    nt = l // tm
    nsteps = b * nt
    nb = n // nsteps
    assert nb * nsteps == n and nb % _V7X_SUBLANES == 0
    cos, sin, decay, qdec, kdec, gc, cos_s, sin_s, gamma1 = tabs

    def wspec(seg):
        return pl.BlockSpec((d, gw), lambda g, bi, t: (0, seg * _RET_HEADS + g))

    def head_tab(rows, cols):
        return pl.BlockSpec((1, rows, cols), lambda g, bi, t: (g, 0, 0))

    state_spec = pl.BlockSpec((None, nb, None, gw, gw),
                              lambda g, bi, t: (layer, bi * nt + t, g, 0, 0))
    in_specs = [pl.BlockSpec((1, tm, d), lambda g, bi, t: (bi, t, 0)),
                pl.BlockSpec((n, d), lambda g, bi, t: (0, 0))]
    in_specs += [wspec(s) for s in range(_IN_SEGMENTS)]
    in_specs += [
        pl.BlockSpec((None, _CONV_K, gw), lambda g, bi, t: (layer, 0, g)),
        pl.BlockSpec((None, 1, gw), lambda g, bi, t: (layer, 0, g)),
        pl.BlockSpec((tm, gw // 2), lambda g, bi, t: (t, 0)),
        pl.BlockSpec((tm, gw // 2), lambda g, bi, t: (t, 0)),
        head_tab(_RET_CHUNK, _RET_CHUNK),
        head_tab(_RET_CHUNK, gw),
        head_tab(_RET_CHUNK, gw),
        head_tab(1, gw),
        pl.BlockSpec((1, gw // 2), lambda g, bi, t: (0, 0)),
        pl.BlockSpec((1, gw // 2), lambda g, bi, t: (0, 0)),
        head_tab(1, gw),
        pl.BlockSpec((None, _CONV_K - 1, n, gw), lambda g, bi, t: (layer, 0, 0, g)),
        state_spec,
    ]
    args = [hp, hs] + [w_in] * _IN_SEGMENTS + [conv_w, gn_g, cos, sin, decay, qdec, kdec, gc,
                                                cos_s, sin_s, gamma1, state_conv_t, state_ret]
    aliases = {}
    if nret_prev is not None:
        for buf, out_idx in zip(nret_prev, (3, 7)):
            in_specs.append(pl.BlockSpec(memory_space=pl.ANY))
            args.append(buf)
            aliases[len(args) - 1] = out_idx
    out_specs = [
        pl.BlockSpec((1, tm, gw), lambda g, bi, t: (bi, t, g)),
        pl.BlockSpec((1, tm, gw), lambda g, bi, t: (bi, t, g)),
        pl.BlockSpec((1, _CONV_K - 1, gw), lambda g, bi, t: (bi, 0, g)),
        pl.BlockSpec((None, 1, 1, gw, gw), lambda g, bi, t: (layer, bi, g, 0, 0)),
        pl.BlockSpec((n, gw), lambda g, bi, t: (0, g)),
        pl.BlockSpec((n, gw), lambda g, bi, t: (0, g)),
        pl.BlockSpec((_CONV_K - 1, n, gw), lambda g, bi, t: (0, 0, g)),
        state_spec,
    ]
    out_shape = [
        jax.ShapeDtypeStruct((b, l, width), _BF16),
        jax.ShapeDtypeStruct((b, l, width), _BF16),
        jax.ShapeDtypeStruct((b, _CONV_K - 1, width), _F32),
        jax.ShapeDtypeStruct((depth, b, _RET_HEADS, gw, gw), _F32),
        jax.ShapeDtypeStruct((n, width), _BF16),
        jax.ShapeDtypeStruct((n, width), _BF16),
        jax.ShapeDtypeStruct((_CONV_K - 1, n, width), _F32),
        jax.ShapeDtypeStruct((depth, n, _RET_HEADS, gw, gw), _F32),
    ]
    scratch_shapes = [
        pltpu.VMEM((_V7X_SUBLANES, gw), _F32),
        pltpu.VMEM((nsteps, gw, nb), _F32),
        pltpu.VMEM((nsteps, gw, nb), _F32),
        pltpu.VMEM((n, gw), _F32),
        pltpu.VMEM((n, gw), _F32),
        pltpu.VMEM((n, gw), _F32),
        pltpu.VMEM((nb, gw), _F32),
    ]
    return pl.pallas_call(
        _mixer_kernel,
        grid=(_RET_HEADS, b, nt),
        in_specs=in_specs,
        out_specs=out_specs,
        out_shape=out_shape,
        scratch_shapes=scratch_shapes,
        input_output_aliases=aliases,
        compiler_params=pltpu.CompilerParams(
            dimension_semantics=("arbitrary", "arbitrary", "arbitrary"),
            vmem_limit_bytes=_V7X_VMEM_LIMIT_BYTES),
        name="mixer",
    )(*args)


def _merge_rows(x_ref, yc_ref, yr_ref, p_ref, wo_ref, wpg_ref, wple_ref, g_ref, out_refs):
    half = yc_ref.shape[-1]
    x1 = x_ref[...] + _mm(yc_ref[...], wo_ref[0:half, :]) + _mm(yr_ref[...], wo_ref[half:, :])
    gate = _sigmoid(_mm(x1.astype(_BF16), wpg_ref[...]))
    x2 = x1 + gate * _mm(p_ref[...].astype(_BF16), wple_ref[...])
    if len(out_refs) == 2:
        out_refs[0][...] = x2
    hn_ref = out_refs[-1]
    hn_ref[...] = _rmsnorm(x2, g_ref[...]).astype(hn_ref.dtype)


def _merge_kernel(xp_ref, ycp_ref, yrp_ref, pp_ref, xs_ref, ycs_ref, yrs_ref, ps_ref,
                  wo_ref, wpg_ref, wple_ref, g_ref, *refs, last):
    r = pl.program_id(0)
    n_prompt_steps = pl.num_programs(0) - 1
    if last:
        prompt_outs, sample_outs = refs[0:1], refs[1:2]
    else:
        nw_refs, refs = refs[:4], refs[4:]
        prompt_outs, sample_outs, cw_refs = refs[0:2], refs[2:4], refs[4:8]

    @pl.when(r < n_prompt_steps)
    def _():
        _merge_rows(xp_ref, ycp_ref, yrp_ref, pp_ref, wo_ref, wpg_ref, wple_ref, g_ref, prompt_outs)
        if not last:
            for src, dst in zip(nw_refs[:3], cw_refs[:3]):
                dst[...] = src[...].astype(dst.dtype)

    @pl.when(r == n_prompt_steps)
    def _():
        _merge_rows(xs_ref, ycs_ref, yrs_ref, ps_ref, wo_ref, wpg_ref, wple_ref, g_ref, sample_outs)
        if not last:
            cw_refs[3][...] = nw_refs[3][...].astype(cw_refs[3].dtype)


def _merge_call(layer, xp, ycp, yrp, pp, xs, ycs, yrs, ps, w_out_b, w_pg_b, w_ple_b, g_next,
                next_f32, rows, last):
    n, d = xp.shape
    ns = xs.shape[0]
    half = ycp.shape[-1]
    pd = pp.shape[-1]
    steps = n // rows
    const = dict(pipeline_mode=pl.Buffered(1))

    def prow(r):
        return jnp.minimum(r, steps - 1)

    in_specs = [
        pl.BlockSpec((rows, d), lambda r: (prow(r), 0)),
        pl.BlockSpec((rows, half), lambda r: (prow(r), 0)),
        pl.BlockSpec((rows, half), lambda r: (prow(r), 0)),
        pl.BlockSpec((None, rows, pd), lambda r: (layer, prow(r), 0)),
        pl.BlockSpec((ns, d), lambda r: (0, 0)),
        pl.BlockSpec((ns, half), lambda r: (0, 0)),
        pl.BlockSpec((ns, half), lambda r: (0, 0)),
        pl.BlockSpec((None, ns, pd), lambda r: (layer, 0, 0)),
        pl.BlockSpec((d, d), lambda r: (0, 0), **const),
        pl.BlockSpec((d, d), lambda r: (0, 0), **const),
        pl.BlockSpec((pd, d), lambda r: (0, 0), **const),
        pl.BlockSpec((1, d), lambda r: (0, 0)),
    ]
    args = [xp, ycp, yrp, pp, xs, ycs, yrs, ps, w_out_b, w_pg_b, w_ple_b, g_next]
    p_spec = pl.BlockSpec((rows, d), lambda r: (prow(r), 0))
    s_spec = pl.BlockSpec((ns, d), lambda r: (0, 0))
    if last:
        out_specs = [p_spec, s_spec]
        out_shape = [jax.ShapeDtypeStruct((n, d), _F32), jax.ShapeDtypeStruct((ns, d), _F32)]
    else:
        out_specs = [p_spec, p_spec, s_spec, s_spec]
        out_shape = [jax.ShapeDtypeStruct((n, d), _F32), jax.ShapeDtypeStruct((n, d), _BF16),
                     jax.ShapeDtypeStruct((ns, d), _F32), jax.ShapeDtypeStruct((ns, d), _BF16)]
        for w in next_f32[:3]:
            slab = w.shape[1] // steps
            in_specs.append(pl.BlockSpec((None, slab, w.shape[2]), lambda r: (layer + 1, prow(r), 0)))
            out_specs.append(pl.BlockSpec((slab, w.shape[2]), lambda r: (prow(r), 0)))
            out_shape.append(jax.ShapeDtypeStruct(w.shape[1:], _BF16))
        w = next_f32[3]
        in_specs.append(pl.BlockSpec((None,) + w.shape[1:], lambda r: (layer + 1, 0, 0), **const))
        out_specs.append(pl.BlockSpec(w.shape[1:], lambda r: (0, 0)))
        out_shape.append(jax.ShapeDtypeStruct(w.shape[1:], _BF16))
        args += list(next_f32)
    return pl.pallas_call(
        functools.partial(_merge_kernel, last=last),
        grid=(steps + 1,),
        in_specs=in_specs,
        out_specs=out_specs,
        out_shape=out_shape,
        compiler_params=pltpu.CompilerParams(
            dimension_semantics=("arbitrary",), vmem_limit_bytes=_V7X_VMEM_LIMIT_BYTES),
        name="merge_last" if last else "merge",
    )(*args)


def _rope_tables(pos, head_dim):
    half = head_dim // 2
    inv = jnp.power(_ROPE_BASE, -jnp.arange(half, dtype=_F32) / half)
    ang = pos.astype(_F32)[:, None] * inv[None, :]
    return jnp.cos(ang), jnp.sin(ang)


def _retention_tables(head_dim):
    c = _RET_CHUNK
    log_gamma = jnp.log1p(-jnp.exp2(-5.0 - jnp.arange(_RET_HEADS, dtype=_F32)))
    idx = jnp.arange(c, dtype=_F32)
    diff = idx[:, None] - idx[None, :]
    decay = jnp.where(diff >= 0.0,
                      jnp.exp(log_gamma[:, None, None] * jnp.maximum(diff, 0.0)[None]), 0.0)
    q_dec = jnp.exp(log_gamma[:, None] * (idx[None, :] + 1.0))
    k_dec = jnp.exp(log_gamma[:, None] * (c - 1.0 - idx[None, :]))
    qdec = jnp.broadcast_to(q_dec[:, :, None], (_RET_HEADS, c, head_dim))
    kdec = jnp.broadcast_to(k_dec[:, :, None], (_RET_HEADS, c, head_dim))
    gc = jnp.broadcast_to(jnp.exp(log_gamma * c)[:, None, None], (_RET_HEADS, 1, head_dim))
    gamma1 = jnp.broadcast_to(jnp.exp(log_gamma * 1.0)[:, None, None], (_RET_HEADS, 1, head_dim))
    return decay, qdec, kdec, gc, gamma1


def kernel(x_prompt, x_sample, state_conv, state_ret, p_prompt, p_sample,
           norm_g, w_in, conv_w, gn_g, w_out, w_pg, w_ple, final_g):
    batch, seq, d = x_prompt.shape
    dec_batch, dec_seq, _ = x_sample.shape
    depth = norm_g.shape[0]
    width = conv_w.shape[-1]
    head_dim = width // _RET_HEADS
    assert dec_seq == 1 and state_conv.shape[2] == _CONV_K - 1
    assert seq % _RET_CHUNK == 0

    tm = 1024
    rows = 256
    norm_rows = 1024

    weights_f32 = (w_in, w_out, w_pg, w_ple)
    w_in_b, w_out_b, w_pg_b, w_ple_b = (w[0].astype(_BF16) for w in weights_f32)

    cos_p, sin_p = _rope_tables(jnp.arange(seq, dtype=jnp.int32), head_dim)
    cos_s, sin_s = _rope_tables(_PAST_LEN + jnp.arange(dec_seq, dtype=jnp.int32), head_dim)
    decay, qdec, kdec, gc, gamma1 = _retention_tables(head_dim)
    tabs = (cos_p, sin_p, decay, qdec, kdec, gc, cos_s, sin_s, gamma1)

    norm_g3 = norm_g.reshape(depth, 1, d)
    gn_g3 = gn_g.reshape(depth, 1, width)
    final_g2 = final_g.reshape(1, d)
    p_prompt2 = p_prompt.reshape(depth, batch * seq, -1)
    p_sample2 = p_sample.reshape(depth, dec_batch * dec_seq, -1)
    state_conv_t = state_conv.transpose(0, 2, 1, 3)

    xp = x_prompt.reshape(batch * seq, d)
    xs = x_sample.reshape(dec_batch * dec_seq, d)
    hp = _norm_call(xp, norm_g3[0], norm_rows)
    hs = _norm_call(xs, norm_g3[0], dec_batch)

    conv_p, conv_s = [], []
    nret = None
    for i in range(depth):
        last = i == depth - 1
        g_next = final_g2 if last else norm_g3[i + 1]
        ycp, yrp, nconv_p, nret_p, ycs, yrs, nconv_s, nret_s = _mixer_call(
            i, hp.reshape(batch, seq, d), hs, w_in_b, conv_w, gn_g3, tabs,
            state_conv_t, state_ret, nret, tm)
        nret = (nret_p, nret_s)
        conv_p.append(nconv_p)
        conv_s.append(nconv_s)
        outs = _merge_call(i, xp, ycp.reshape(batch * seq, width), yrp.reshape(batch * seq, width),
                           p_prompt2, xs, ycs, yrs, p_sample2,
                           w_out_b, w_pg_b, w_ple_b, g_next, weights_f32, rows, last)
        if last:
            y_prompt, y_sample = outs
        else:
            xp, hp, xs, hs, w_in_b, w_out_b, w_pg_b, w_ple_b = outs

    return (y_prompt.reshape(batch, seq, d),
            y_sample.reshape(dec_batch, dec_seq, d),
            jnp.stack(conv_p), nret[0],
            jnp.stack(conv_s).transpose(0, 2, 1, 3), nret[1])
```

```python
import functools

import jax
import jax.numpy as jnp
from jax import lax
from jax.experimental import pallas as pl
from jax.experimental.pallas import tpu as pltpu

_ROPE_BASE = 10000.0
_NORM_EPS = 1e-6
_GN_EPS = 1e-6
_CONV_K = 3
_RET_HEADS = 4
_RET_CHUNK = 256
_PAST_LEN = 16384
_IN_SEGMENTS = 8

_V7X_SUBLANES = 8
_V7X_VMEM_LIMIT_BYTES = 62 * 1024 * 1024

_BF16 = jnp.bfloat16
_F32 = jnp.float32


def _mm(a, b):
    return jnp.dot(a, b, preferred_element_type=_F32)


def _mm_nt(a, b):
    return lax.dot_general(a, b, (((1,), (1,)), ((), ())), preferred_element_type=_F32)


_sigmoid = jax.nn.sigmoid
_silu = jax.nn.silu


def _rmsnorm(x, g):
    return x * lax.rsqrt(jnp.mean(x * x, axis=-1, keepdims=True) + _NORM_EPS) * g


def _rope(x, cos, sin):
    half = x.shape[-1] // 2
    x1, x2 = x[:, :half], x[:, half:]
    return jnp.concatenate([x1 * cos - x2 * sin, x1 * sin + x2 * cos], axis=-1)


def _group_norm(o):
    mu = jnp.mean(o, axis=-1, keepdims=True)
    d = o - mu
    var = jnp.mean(d * d, axis=-1, keepdims=True)
    return d * lax.rsqrt(var + _GN_EPS)


def _norm_kernel(x_ref, g_ref, h_ref):
    h_ref[...] = _rmsnorm(x_ref[...], g_ref[...]).astype(h_ref.dtype)


def _norm_call(x, g_layers, rows):
    n, d = x.shape
    return pl.pallas_call(
        _norm_kernel,
        grid=(n // rows,),
        in_specs=[pl.BlockSpec((rows, d), lambda r: (r, 0)),
                  pl.BlockSpec((None, 1, d), lambda r: (0, 0, 0))],
        out_specs=pl.BlockSpec((rows, d), lambda r: (r, 0)),
        out_shape=jax.ShapeDtypeStruct((n, d), _BF16),
        compiler_params=pltpu.CompilerParams(
            dimension_semantics=("arbitrary",), vmem_limit_bytes=_V7X_VMEM_LIMIT_BYTES),
        name="rmsnorm_rows",
    )(x, g_layers)


def _mixer_kernel(hp_ref, hs_ref, wb_ref, wc_ref, wx_ref, wg_ref, wq_ref, wk_ref, wv_ref, wr_ref,
                  convw_ref, gng_ref, cos_ref, sin_ref, cos_s_ref, sin_s_ref, buf_ref, s_ref, *rest):
    (yc_ref, yr_ref, nconv_ref, nret_ref, ycs_ref, yrs_ref, nconv_s_ref, nret_s_ref,
     tail_ref, qt_ref, kt_ref, vs_ref, zrs_ref, os_ref, ostep_ref) = rest[-15:]
    b = pl.program_id(1)
    t = pl.program_id(2)
    nt = pl.num_programs(2)
    step = b * nt + t
    tm = hp_ref.shape[1]
    nsteps, gw, nb = qt_ref.shape
    scale = gw ** -0.5
    cw = convw_ref[...]
    gng = gng_ref[...]

    @pl.when(step == 0)
    def _():
        hs = hs_ref[...]
        u = _mm(hs, wc_ref[...]) * _mm(hs, wx_ref[...])
        conv = cw[0:1] * buf_ref[0] + cw[1:2] * buf_ref[1] + cw[2:3] * u
        ycs_ref[...] = (_mm(hs, wb_ref[...]) * conv
                        * _silu(_mm(hs, wg_ref[...]))).astype(ycs_ref.dtype)
        nconv_s_ref[0] = buf_ref[1]
        nconv_s_ref[1] = u
        cos = cos_s_ref[...]
        sin = sin_s_ref[...]
        qr = _rope(_mm(hs, wq_ref[...]), cos, sin)
        kr = _rope(_mm(hs, wk_ref[...]), cos, sin) * scale
        vs_ref[...] = _mm(hs, wv_ref[...])
        zrs_ref[...] = _mm(hs, wr_ref[...])
        for st in range(nsteps):
            qt_ref[st] = qr[st * nb:(st + 1) * nb].T
            kt_ref[st] = kr[st * nb:(st + 1) * nb].T

    @pl.when(t == 0)
    def _():
        tail_ref[...] = jnp.zeros_like(tail_ref)
        nret_ref[...] = jnp.zeros_like(nret_ref)


    row0 = pl.multiple_of(step * nb, nb)
    q_t = qt_ref[step]
    k_t = kt_ref[step]
    v_rows = vs_ref[pl.ds(row0, nb), :]
    c = _RET_CHUNK
    head = pl.program_id(0).astype(_F32)
    log_gamma = jnp.log1p(-jnp.exp2(-5.0 - (jnp.zeros((c, gw), _F32) + head)))
    log_gamma_row = jnp.log1p(-jnp.exp2(-5.0 - (jnp.zeros((1, gw), _F32) + head)))
    gamma = jnp.exp(log_gamma_row)
    for i in range(nb):
        s_new = gamma * s_ref[i] + k_t[:, i:i + 1] * v_rows[i:i + 1, :]
        nret_s_ref[i] = s_new
        ostep_ref[i:i + 1, :] = jnp.sum(q_t[:, i:i + 1] * s_new, axis=0, keepdims=True)
    os_ref[pl.ds(row0, nb), :] = ostep_ref[...]

    h = hp_ref[0]
    zq = _mm(h, wq_ref[...])
    zk = _mm(h, wk_ref[...])
    zv = _mm(h, wv_ref[...])
    zr = _mm(h, wr_ref[...])
    row_i = lax.broadcasted_iota(jnp.int32, (c, gw), 0).astype(_F32)
    col_i = lax.broadcasted_iota(jnp.int32, (c, gw), 1).astype(_F32)
    diff = row_i - col_i
    decay = jnp.where(diff >= 0.0, jnp.exp(log_gamma * jnp.maximum(diff, 0.0)), 0.0)
    qdec = jnp.exp(log_gamma * (row_i + 1.0))
    kdec = jnp.exp(log_gamma * (c - 1.0 - row_i))
    gc = jnp.exp(log_gamma_row * c)
    n_chunks = tm // c

    qbs, vs, scores, incs = [], [], [], []
    for j in range(n_chunks):
        sl = slice(j * c, (j + 1) * c)
        cos = cos_ref[sl, :]
        sin = sin_ref[sl, :]
        qr = _rope(zq[sl], cos, sin)
        kr = _rope(zk[sl], cos, sin) * scale
        v = zv[sl].astype(_BF16)
        qb = qr.astype(_BF16)
        qbs.append(qb)
        vs.append(v)
        scores.append(_mm_nt(qb, kr.astype(_BF16)))
        incs.append(_mm((kr * kdec).T.astype(_BF16), v))

    u = _mm(h, wc_ref[...]) * _mm(h, wx_ref[...])
    prev = tail_ref[...]
    row = lax.broadcasted_iota(jnp.int32, u.shape, 0)
    p1 = prev[_V7X_SUBLANES - 1:_V7X_SUBLANES]
    p2 = prev[_V7X_SUBLANES - 2:_V7X_SUBLANES - 1]
    s1 = jnp.where(row == 0, p1, pltpu.roll(u, 1, 0))
    s2 = jnp.where(row == 0, p2, jnp.where(row == 1, p1, pltpu.roll(u, 2, 0)))
    conv = cw[0:1] * s2 + cw[1:2] * s1 + cw[2:3] * u
    tail_ref[...] = u[tm - _V7X_SUBLANES:tm]

    s_state = nret_ref[0, 0]
    for j in range(n_chunks):
        sl = slice(j * c, (j + 1) * c)
        intra = _mm((scores[j] * decay).astype(_BF16), vs[j])
        inter = _mm(qbs[j], s_state.astype(_BF16)) * qdec
        s_state = gc * s_state + incs[j]
        on = _group_norm(intra + inter)
        yr_ref[0, sl, :] = (on * gng * _silu(zr[sl])).astype(yr_ref.dtype)
    nret_ref[0, 0] = s_state

    gated = _mm(h, wb_ref[...]) * conv
    split = tm - tm // 4
    for rs in (slice(0, split), slice(split, tm)):
        zg = _mm(h[rs], wg_ref[...])
        yc_ref[0, rs, :] = (gated[rs] * _silu(zg)).astype(yc_ref.dtype)

    @pl.when(t == nt - 1)
    def _():
        nconv_ref[0] = tail_ref[_V7X_SUBLANES - (_CONV_K - 1):_V7X_SUBLANES, :]

    @pl.when(step == nsteps - 1)
    def _():
        on = _group_norm(os_ref[...])
        yrs_ref[...] = (on * gng * _silu(zrs_ref[...])).astype(yrs_ref.dtype)


def _mixer_call(layer, hp, hs, w_in, conv_w, gn_g, tabs, state_conv_t, state_ret, nret_prev, tm):
    b, l, d = hp.shape
    n = hs.shape[0]
    depth = state_ret.shape[0]
    width = conv_w.shape[-1]
    gw = width // _RET_HEADS
    nt = l // tm
    nsteps = b * nt
    nb = n // nsteps
    assert nb * nsteps == n and nb % _V7X_SUBLANES == 0
    assert gw == _RET_CHUNK and tm % _RET_CHUNK == 0
    cos, sin, cos_s, sin_s = tabs

    def wspec(seg):
        return pl.BlockSpec((d, gw), lambda g, bi, t: (0, seg * _RET_HEADS + g))

    state_spec = pl.BlockSpec((None, nb, None, gw, gw),
                              lambda g, bi, t: (layer, bi * nt + t, g, 0, 0))
    in_specs = [pl.BlockSpec((1, tm, d), lambda g, bi, t: (bi, t, 0)),
                pl.BlockSpec((n, d), lambda g, bi, t: (0, 0))]
    in_specs += [wspec(s) for s in range(_IN_SEGMENTS)]
    in_specs += [
        pl.BlockSpec((None, _CONV_K, gw), lambda g, bi, t: (layer, 0, g)),
        pl.BlockSpec((None, 1, gw), lambda g, bi, t: (layer, 0, g)),
        pl.BlockSpec((tm, gw // 2), lambda g, bi, t: (t, 0)),
        pl.BlockSpec((tm, gw // 2), lambda g, bi, t: (t, 0)),
        pl.BlockSpec((1, gw // 2), lambda g, bi, t: (0, 0)),
        pl.BlockSpec((1, gw // 2), lambda g, bi, t: (0, 0)),
        pl.BlockSpec((None, _CONV_K - 1, n, gw), lambda g, bi, t: (layer, 0, 0, g)),
        state_spec,
    ]
    args = [hp, hs] + [w_in] * _IN_SEGMENTS + [conv_w, gn_g, cos, sin, cos_s, sin_s,
                                                state_conv_t, state_ret]
    aliases = {}
    if nret_prev is not None:
        for buf, out_idx in zip(nret_prev, (3, 7)):
            in_specs.append(pl.BlockSpec(memory_space=pl.ANY))
            args.append(buf)
            aliases[len(args) - 1] = out_idx
    out_specs = [
        pl.BlockSpec((1, tm, gw), lambda g, bi, t: (bi, t, g)),
        pl.BlockSpec((1, tm, gw), lambda g, bi, t: (bi, t, g)),
        pl.BlockSpec((1, _CONV_K - 1, gw), lambda g, bi, t: (bi, 0, g)),
        pl.BlockSpec((None, 1, 1, gw, gw), lambda g, bi, t: (layer, bi, g, 0, 0)),
        pl.BlockSpec((n, gw), lambda g, bi, t: (0, g)),
        pl.BlockSpec((n, gw), lambda g, bi, t: (0, g)),
        pl.BlockSpec((_CONV_K - 1, n, gw), lambda g, bi, t: (0, 0, g)),
        state_spec,
    ]
    out_shape = [
        jax.ShapeDtypeStruct((b, l, width), _BF16),
        jax.ShapeDtypeStruct((b, l, width), _BF16),
        jax.ShapeDtypeStruct((b, _CONV_K - 1, width), _F32),
        jax.ShapeDtypeStruct((depth, b, _RET_HEADS, gw, gw), _F32),
        jax.ShapeDtypeStruct((n, width), _BF16),
        jax.ShapeDtypeStruct((n, width), _BF16),
        jax.ShapeDtypeStruct((_CONV_K - 1, n, width), _F32),
        jax.ShapeDtypeStruct((depth, n, _RET_HEADS, gw, gw), _F32),
    ]
    scratch_shapes = [
        pltpu.VMEM((_V7X_SUBLANES, gw), _F32),
        pltpu.VMEM((nsteps, gw, nb), _F32),
        pltpu.VMEM((nsteps, gw, nb), _F32),
        pltpu.VMEM((n, gw), _F32),
        pltpu.VMEM((n, gw), _F32),
        pltpu.VMEM((n, gw), _F32),
        pltpu.VMEM((nb, gw), _F32),
    ]
    return pl.pallas_call(
        _mixer_kernel,
        grid=(_RET_HEADS, b, nt),
        in_specs=in_specs,
        out_specs=out_specs,
        out_shape=out_shape,
        scratch_shapes=scratch_shapes,
        input_output_aliases=aliases,
        compiler_params=pltpu.CompilerParams(
            dimension_semantics=("arbitrary", "arbitrary", "arbitrary"),
            vmem_limit_bytes=_V7X_VMEM_LIMIT_BYTES),
        name="mixer",
    )(*args)


def _merge_rows(x_ref, yc_ref, yr_ref, p_ref, wo_ref, wpg_ref, wple_ref, g_ref, out_refs):
    half = yc_ref.shape[-1]
    x1 = x_ref[...] + _mm(yc_ref[...], wo_ref[0:half, :]) + _mm(yr_ref[...], wo_ref[half:, :])
    gate = _sigmoid(_mm(x1.astype(_BF16), wpg_ref[...]))
    x2 = x1 + gate * _mm(p_ref[...].astype(_BF16), wple_ref[...])
    if len(out_refs) == 2:
        out_refs[0][...] = x2
    hn_ref = out_refs[-1]
    hn_ref[...] = _rmsnorm(x2, g_ref[...]).astype(hn_ref.dtype)


def _merge_kernel(xp_ref, ycp_ref, yrp_ref, pp_ref, xs_ref, ycs_ref, yrs_ref, ps_ref,
                  wo_ref, wpg_ref, wple_ref, g_ref, *refs, last):
    r = pl.program_id(0)
    n_prompt_steps = pl.num_programs(0) - 1
    if last:
        prompt_outs, sample_outs = refs[0:1], refs[1:2]
    else:
        nw_refs, refs = refs[:4], refs[4:]
        prompt_outs, sample_outs, cw_refs = refs[0:2], refs[2:4], refs[4:8]

    @pl.when(r < n_prompt_steps)
    def _():
        _merge_rows(xp_ref, ycp_ref, yrp_ref, pp_ref, wo_ref, wpg_ref, wple_ref, g_ref, prompt_outs)
        if not last:
            for src, dst in zip(nw_refs[:3], cw_refs[:3]):
                dst[...] = src[...].astype(dst.dtype)

    @pl.when(r == n_prompt_steps)
    def _():
        _merge_rows(xs_ref, ycs_ref, yrs_ref, ps_ref, wo_ref, wpg_ref, wple_ref, g_ref, sample_outs)
        if not last:
            cw_refs[3][...] = nw_refs[3][...].astype(cw_refs[3].dtype)


def _merge_call(layer, xp, ycp, yrp, pp, xs, ycs, yrs, ps, w_out_b, w_pg_b, w_ple_b, g_next,
                next_f32, rows, last):
    n, d = xp.shape
    ns = xs.shape[0]
    half = ycp.shape[-1]
    pd = pp.shape[-1]
    steps = n // rows
    const = dict(pipeline_mode=pl.Buffered(1))

    def prow(r):
        return jnp.minimum(r, steps - 1)

    in_specs = [
        pl.BlockSpec((rows, d), lambda r: (prow(r), 0)),
        pl.BlockSpec((rows, half), lambda r: (prow(r), 0)),
        pl.BlockSpec((rows, half), lambda r: (prow(r), 0)),
        pl.BlockSpec((None, rows, pd), lambda r: (layer, prow(r), 0)),
        pl.BlockSpec((ns, d), lambda r: (0, 0)),
        pl.BlockSpec((ns, half), lambda r: (0, 0)),
        pl.BlockSpec((ns, half), lambda r: (0, 0)),
        pl.BlockSpec((None, ns, pd), lambda r: (layer, 0, 0)),
        pl.BlockSpec((d, d), lambda r: (0, 0), **const),
        pl.BlockSpec((d, d), lambda r: (0, 0), **const),
        pl.BlockSpec((pd, d), lambda r: (0, 0), **const),
        (pl.BlockSpec((1, d), lambda r: (0, 0)) if last else
         pl.BlockSpec((None, 1, d), lambda r: (layer + 1, 0, 0))),
    ]
    args = [xp, ycp, yrp, pp, xs, ycs, yrs, ps, w_out_b, w_pg_b, w_ple_b, g_next]
    p_spec = pl.BlockSpec((rows, d), lambda r: (prow(r), 0))
    s_spec = pl.BlockSpec((ns, d), lambda r: (0, 0))
    if last:
        out_specs = [p_spec, s_spec]
        out_shape = [jax.ShapeDtypeStruct((n, d), _F32), jax.ShapeDtypeStruct((ns, d), _F32)]
    else:
        out_specs = [p_spec, p_spec, s_spec, s_spec]
        out_shape = [jax.ShapeDtypeStruct((n, d), _F32), jax.ShapeDtypeStruct((n, d), _BF16),
                     jax.ShapeDtypeStruct((ns, d), _F32), jax.ShapeDtypeStruct((ns, d), _BF16)]
        for w in next_f32[:3]:
            slab = w.shape[1] // steps
            in_specs.append(pl.BlockSpec((None, slab, w.shape[2]), lambda r: (layer + 1, prow(r), 0)))
            out_specs.append(pl.BlockSpec((slab, w.shape[2]), lambda r: (prow(r), 0)))
            out_shape.append(jax.ShapeDtypeStruct(w.shape[1:], _BF16))
        w = next_f32[3]
        in_specs.append(pl.BlockSpec((None,) + w.shape[1:], lambda r: (layer + 1, 0, 0), **const))
        out_specs.append(pl.BlockSpec(w.shape[1:], lambda r: (0, 0)))
        out_shape.append(jax.ShapeDtypeStruct(w.shape[1:], _BF16))
        args += list(next_f32)
    return pl.pallas_call(
        functools.partial(_merge_kernel, last=last),
        grid=(steps + 1,),
        in_specs=in_specs,
        out_specs=out_specs,
        out_shape=out_shape,
        compiler_params=pltpu.CompilerParams(
            dimension_semantics=("arbitrary",), vmem_limit_bytes=_V7X_VMEM_LIMIT_BYTES),
        name="merge_last" if last else "merge",
    )(*args)


def _rope_tables(pos, head_dim):
    half = head_dim // 2
    inv = jnp.power(_ROPE_BASE, -jnp.arange(half, dtype=_F32) / half)
    ang = pos.astype(_F32)[:, None] * inv[None, :]
    return jnp.cos(ang), jnp.sin(ang)


def kernel(x_prompt, x_sample, state_conv, state_ret, p_prompt, p_sample,
           norm_g, w_in, conv_w, gn_g, w_out, w_pg, w_ple, final_g):
    batch, seq, d = x_prompt.shape
    dec_batch, dec_seq, _ = x_sample.shape
    depth = norm_g.shape[0]
    width = conv_w.shape[-1]
    head_dim = width // _RET_HEADS
    assert dec_seq == 1 and state_conv.shape[2] == _CONV_K - 1
    assert seq % _RET_CHUNK == 0

    tm = 1024
    rows = 256
    norm_rows = 1024

    weights_f32 = (w_in, w_out, w_pg, w_ple)
    w_in_b, w_out_b, w_pg_b, w_ple_b = (w[0].astype(_BF16) for w in weights_f32)

    cos_p, sin_p = _rope_tables(jnp.arange(seq, dtype=jnp.int32), head_dim)
    cos_s, sin_s = _rope_tables(_PAST_LEN + jnp.arange(dec_seq, dtype=jnp.int32), head_dim)
    tabs = (cos_p, sin_p, cos_s, sin_s)

    norm_g3 = norm_g.reshape(depth, 1, d)
    gn_g3 = gn_g.reshape(depth, 1, width)
    final_g2 = final_g.reshape(1, d)
    p_prompt2 = p_prompt.reshape(depth, batch * seq, -1)
    p_sample2 = p_sample.reshape(depth, dec_batch * dec_seq, -1)
    state_conv_t = state_conv.transpose(0, 2, 1, 3)

    xp = x_prompt.reshape(batch * seq, d)
    xs = x_sample.reshape(dec_batch * dec_seq, d)
    hp = _norm_call(xp, norm_g3, norm_rows)
    hs = _norm_call(xs, norm_g3, dec_batch)

    conv_p, conv_s = [], []
    nret = None
    for i in range(depth):
        last = i == depth - 1
        g_next = final_g2 if last else norm_g3
        ycp, yrp, nconv_p, nret_p, ycs, yrs, nconv_s, nret_s = _mixer_call(
            i, hp.reshape(batch, seq, d), hs, w_in_b, conv_w, gn_g3, tabs,
            state_conv_t, state_ret, nret, tm)
        nret = (nret_p, nret_s)
        conv_p.append(nconv_p)
        conv_s.append(nconv_s)
        outs = _merge_call(i, xp, ycp.reshape(batch * seq, width), yrp.reshape(batch * seq, width),
                           p_prompt2, xs, ycs, yrs, p_sample2,
                           w_out_b, w_pg_b, w_ple_b, g_next, weights_f32, rows, last)
        if last:
            y_prompt, y_sample = outs
        else:
            xp, hp, xs, hs, w_in_b, w_out_b, w_pg_b, w_ple_b = outs

    return (y_prompt.reshape(batch, seq, d),
            y_sample.reshape(dec_batch, dec_seq, d),
            jnp.stack(conv_p), nret[0],
            jnp.stack(conv_s).transpose(0, 2, 1, 3), nret[1])
```

```python
import functools

import jax
import jax.numpy as jnp
from jax import lax
from jax.experimental import pallas as pl
from jax.experimental.pallas import tpu as pltpu

_ROPE_BASE = 10000.0
_NORM_EPS = 1e-6
_GN_EPS = 1e-6
_CONV_K = 3
_RET_HEADS = 4
_RET_CHUNK = 256
_PAST_LEN = 16384
_IN_SEGMENTS = 8

_V7X_SUBLANES = 8
_V7X_VMEM_LIMIT_BYTES = 62 * 1024 * 1024

_BF16 = jnp.bfloat16
_F32 = jnp.float32


def _mm(a, b):
    return jnp.dot(a, b, preferred_element_type=_F32)


def _mm_nt(a, b):
    return lax.dot_general(a, b, (((1,), (1,)), ((), ())), preferred_element_type=_F32)


_sigmoid = jax.nn.sigmoid
_silu = jax.nn.silu


def _rmsnorm(x, g):
    return x * lax.rsqrt(jnp.mean(x * x, axis=-1, keepdims=True) + _NORM_EPS) * g


def _rope(x, cos, sin):
    half = x.shape[-1] // 2
    x1, x2 = x[:, :half], x[:, half:]
    return jnp.concatenate([x1 * cos - x2 * sin, x1 * sin + x2 * cos], axis=-1)


def _group_norm(o):
    mu = jnp.mean(o, axis=-1, keepdims=True)
    d = o - mu
    var = jnp.mean(d * d, axis=-1, keepdims=True)
    return d * lax.rsqrt(var + _GN_EPS)


def _norm_kernel(x_ref, g_ref, h_ref):
    h_ref[...] = _rmsnorm(x_ref[...], g_ref[...]).astype(h_ref.dtype)


def _norm_call(x, g_layers, rows):
    n, d = x.shape
    return pl.pallas_call(
        _norm_kernel,
        grid=(n // rows,),
        in_specs=[pl.BlockSpec((rows, d), lambda r: (r, 0)),
                  pl.BlockSpec((None, 1, d), lambda r: (0, 0, 0))],
        out_specs=pl.BlockSpec((rows, d), lambda r: (r, 0)),
        out_shape=jax.ShapeDtypeStruct((n, d), _BF16),
        compiler_params=pltpu.CompilerParams(
            dimension_semantics=("arbitrary",), vmem_limit_bytes=_V7X_VMEM_LIMIT_BYTES),
        name="rmsnorm_rows",
    )(x, g_layers)


def _mixer_kernel(hp_ref, hs_ref, wb_ref, wc_ref, wx_ref, wg_ref, wq_ref, wk_ref, wv_ref, wr_ref,
                  convw_ref, gng_ref, cos_ref, sin_ref, cos_s_ref, sin_s_ref, buf_ref, s_ref, *rest):
    (yc_ref, yr_ref, nconv_ref, nret_ref, ycs_ref, yrs_ref, nconv_s_ref, nret_s_ref,
     tail_ref, qt_ref, kt_ref, vs_ref, zrs_ref, os_ref, ostep_ref) = rest[-15:]
    b = pl.program_id(1)
    t = pl.program_id(2)
    nt = pl.num_programs(2)
    step = b * nt + t
    tm = hp_ref.shape[1]
    nsteps, gw, nb = qt_ref.shape
    scale = gw ** -0.5
    cw = convw_ref[...]
    gng = gng_ref[...]

    @pl.when(step == 0)
    def _():
        hs = hs_ref[...]
        u = _mm(hs, wc_ref[...]) * _mm(hs, wx_ref[...])
        conv = cw[0:1] * buf_ref[0] + cw[1:2] * buf_ref[1] + cw[2:3] * u
        ycs_ref[...] = (_mm(hs, wb_ref[...]) * conv
                        * _silu(_mm(hs, wg_ref[...]))).astype(ycs_ref.dtype)
        nconv_s_ref[0] = buf_ref[1]
        nconv_s_ref[1] = u
        cos = cos_s_ref[...]
        sin = sin_s_ref[...]
        qr = _rope(_mm(hs, wq_ref[...]), cos, sin)
        kr = _rope(_mm(hs, wk_ref[...]), cos, sin) * scale
        vs_ref[...] = _mm(hs, wv_ref[...])
        zrs_ref[...] = _mm(hs, wr_ref[...])
        for st in range(nsteps):
            qt_ref[st] = qr[st * nb:(st + 1) * nb].T
            kt_ref[st] = kr[st * nb:(st + 1) * nb].T

    @pl.when(t == 0)
    def _():
        tail_ref[...] = jnp.zeros_like(tail_ref)
        nret_ref[...] = jnp.zeros_like(nret_ref)


    row0 = pl.multiple_of(step * nb, nb)
    q_t = qt_ref[step]
    k_t = kt_ref[step]
    v_rows = vs_ref[pl.ds(row0, nb), :]
    c = _RET_CHUNK
    head = pl.program_id(0).astype(_F32)
    log_gamma = jnp.log1p(-jnp.exp2(-5.0 - (jnp.zeros((c, gw), _F32) + head)))
    log_gamma_row = jnp.log1p(-jnp.exp2(-5.0 - (jnp.zeros((1, gw), _F32) + head)))
    gamma = jnp.exp(log_gamma_row)
    for i in range(nb):
        s_new = gamma * s_ref[i] + k_t[:, i:i + 1] * v_rows[i:i + 1, :]
        nret_s_ref[i] = s_new
        ostep_ref[i:i + 1, :] = jnp.sum(q_t[:, i:i + 1] * s_new, axis=0, keepdims=True)
    os_ref[pl.ds(row0, nb), :] = ostep_ref[...]

    h = hp_ref[0]
    zq = _mm(h, wq_ref[...])
    zk = _mm(h, wk_ref[...])
    zv = _mm(h, wv_ref[...])
    zr = _mm(h, wr_ref[...])
    row_i = lax.broadcasted_iota(jnp.int32, (c, gw), 0).astype(_F32)
    col_i = lax.broadcasted_iota(jnp.int32, (c, gw), 1).astype(_F32)
    diff = row_i - col_i
    decay = jnp.where(diff >= 0.0, jnp.exp(log_gamma * jnp.maximum(diff, 0.0)), 0.0)
    qdec = jnp.exp(log_gamma * (row_i + 1.0))
    kdec = jnp.exp(log_gamma * (c - 1.0 - row_i))
    gc = jnp.exp(log_gamma_row * c)
    n_chunks = tm // c

    qbs, vs, scores, incs = [], [], [], []
    for j in range(n_chunks):
        sl = slice(j * c, (j + 1) * c)
        cos = cos_ref[sl, :]
        sin = sin_ref[sl, :]
        qr = _rope(zq[sl], cos, sin)
        kr = _rope(zk[sl], cos, sin) * scale
        v = zv[sl].astype(_BF16)
        qb = qr.astype(_BF16)
        qbs.append(qb)
        vs.append(v)
        scores.append(_mm_nt(qb, kr.astype(_BF16)))
        incs.append(_mm((kr * kdec).T.astype(_BF16), v))

    u = _mm(h, wc_ref[...]) * _mm(h, wx_ref[...])
    prev = tail_ref[...]
    row = lax.broadcasted_iota(jnp.int32, u.shape, 0)
    p1 = prev[_V7X_SUBLANES - 1:_V7X_SUBLANES]
    p2 = prev[_V7X_SUBLANES - 2:_V7X_SUBLANES - 1]
    s1 = jnp.where(row == 0, p1, pltpu.roll(u, 1, 0))
    s2 = jnp.where(row == 0, p2, jnp.where(row == 1, p1, pltpu.roll(u, 2, 0)))
    conv = cw[0:1] * s2 + cw[1:2] * s1 + cw[2:3] * u
    tail_ref[...] = u[tm - _V7X_SUBLANES:tm]

    s_state = nret_ref[0, 0]
    for j in range(n_chunks):
        sl = slice(j * c, (j + 1) * c)
        intra = _mm((scores[j] * decay).astype(_BF16), vs[j])
        inter = _mm(qbs[j], s_state.astype(_BF16)) * qdec
        s_state = gc * s_state + incs[j]
        on = _group_norm(intra + inter)
        yr_ref[0, sl, :] = (on * gng * _silu(zr[sl])).astype(yr_ref.dtype)
    nret_ref[0, 0] = s_state

    gated = _mm(h, wb_ref[...]) * conv
    split = tm - tm // 4
    for rs in (slice(0, split), slice(split, tm)):
        zg = _mm(h[rs], wg_ref[...])
        yc_ref[0, rs, :] = (gated[rs] * _silu(zg)).astype(yc_ref.dtype)

    @pl.when(t == nt - 1)
    def _():
        nconv_ref[0] = tail_ref[_V7X_SUBLANES - (_CONV_K - 1):_V7X_SUBLANES, :]

    @pl.when(step == nsteps - 1)
    def _():
        on = _group_norm(os_ref[...])
        yrs_ref[...] = (on * gng * _silu(zrs_ref[...])).astype(yrs_ref.dtype)


def _mixer_call(layer, hp, hs, w_in, conv_w, gn_g, tabs, state_conv_t, state_ret, nret_prev, tm):
    b, l, d = hp.shape
    n = hs.shape[0]
    depth = state_ret.shape[0]
    width = conv_w.shape[-1]
    gw = width // _RET_HEADS
    nt = l // tm
    nsteps = b * nt
    nb = n // nsteps
    assert nb * nsteps == n and nb % _V7X_SUBLANES == 0
    assert gw == _RET_CHUNK and tm % _RET_CHUNK == 0
    cos, sin, cos_s, sin_s = tabs

    def wspec(seg):
        return pl.BlockSpec((d, gw), lambda g, bi, t: (0, seg * _RET_HEADS + g))

    state_spec = pl.BlockSpec((None, nb, None, gw, gw),
                              lambda g, bi, t: (layer, bi * nt + t, g, 0, 0))
    in_specs = [pl.BlockSpec((1, tm, d), lambda g, bi, t: (bi, t, 0)),
                pl.BlockSpec((n, d), lambda g, bi, t: (0, 0))]
    in_specs += [wspec(s) for s in range(_IN_SEGMENTS)]
    in_specs += [
        pl.BlockSpec((None, _CONV_K, gw), lambda g, bi, t: (layer, 0, g)),
        pl.BlockSpec((None, 1, gw), lambda g, bi, t: (layer, 0, g)),
        pl.BlockSpec((tm, gw // 2), lambda g, bi, t: (t, 0)),
        pl.BlockSpec((tm, gw // 2), lambda g, bi, t: (t, 0)),
        pl.BlockSpec((1, gw // 2), lambda g, bi, t: (0, 0)),
        pl.BlockSpec((1, gw // 2), lambda g, bi, t: (0, 0)),
        pl.BlockSpec((None, _CONV_K - 1, n, gw), lambda g, bi, t: (layer, 0, 0, g)),
        state_spec,
    ]
    args = [hp, hs] + [w_in] * _IN_SEGMENTS + [conv_w, gn_g, cos, sin, cos_s, sin_s,
                                                state_conv_t, state_ret]
    aliases = {}
    if nret_prev is not None:
        for buf, out_idx in zip(nret_prev, (3, 7)):
            in_specs.append(pl.BlockSpec(memory_space=pl.ANY))
            args.append(buf)
            aliases[len(args) - 1] = out_idx
    out_specs = [
        pl.BlockSpec((1, tm, gw), lambda g, bi, t: (bi, t, g)),
        pl.BlockSpec((1, tm, gw), lambda g, bi, t: (bi, t, g)),
        pl.BlockSpec((1, _CONV_K - 1, gw), lambda g, bi, t: (bi, 0, g)),
        pl.BlockSpec((None, 1, 1, gw, gw), lambda g, bi, t: (layer, bi, g, 0, 0)),
        pl.BlockSpec((n, gw), lambda g, bi, t: (0, g)),
        pl.BlockSpec((n, gw), lambda g, bi, t: (0, g)),
        pl.BlockSpec((_CONV_K - 1, n, gw), lambda g, bi, t: (0, 0, g)),
        state_spec,
    ]
    out_shape = [
        jax.ShapeDtypeStruct((b, l, width), _BF16),
        jax.ShapeDtypeStruct((b, l, width), _BF16),
        jax.ShapeDtypeStruct((b, _CONV_K - 1, width), _F32),
        jax.ShapeDtypeStruct((depth, b, _RET_HEADS, gw, gw), _F32),
        jax.ShapeDtypeStruct((n, width), _BF16),
        jax.ShapeDtypeStruct((n, width), _BF16),
        jax.ShapeDtypeStruct((_CONV_K - 1, n, width), _F32),
        jax.ShapeDtypeStruct((depth, n, _RET_HEADS, gw, gw), _F32),
    ]
    scratch_shapes = [
        pltpu.VMEM((_V7X_SUBLANES, gw), _F32),
        pltpu.VMEM((nsteps, gw, nb), _F32),
        pltpu.VMEM((nsteps, gw, nb), _F32),
        pltpu.VMEM((n, gw), _F32),
        pltpu.VMEM((n, gw), _F32),
        pltpu.VMEM((n, gw), _F32),
        pltpu.VMEM((nb, gw), _F32),
    ]
    return pl.pallas_call(
        _mixer_kernel,
        grid=(_RET_HEADS, b, nt),
        in_specs=in_specs,
        out_specs=out_specs,
        out_shape=out_shape,
        scratch_shapes=scratch_shapes,
        input_output_aliases=aliases,
        compiler_params=pltpu.CompilerParams(
            dimension_semantics=("arbitrary", "arbitrary", "arbitrary"),
            vmem_limit_bytes=_V7X_VMEM_LIMIT_BYTES),
        name="mixer",
    )(*args)


def _merge_rows(x_ref, yc_ref, yr_ref, p_ref, wo_ref, wpg_ref, wple_ref, g_ref, out_refs,
                arrive=None):
    def ready(i):
        if arrive is not None:
            arrive[i].wait()

    half = yc_ref.shape[-1]
    ready(0)
    x1 = x_ref[...] + _mm(yc_ref[...], wo_ref[0:half, :])
    ready(1)
    x1 = x1 + _mm(yr_ref[...], wo_ref[half:, :])
    ready(2)
    gate = _sigmoid(_mm(x1.astype(_BF16), wpg_ref[...]))
    ready(3)
    x2 = x1 + gate * _mm(p_ref[...].astype(_BF16), wple_ref[...])
    if len(out_refs) == 2:
        out_refs[0][...] = x2
    hn_ref = out_refs[-1]
    hn_ref[...] = _rmsnorm(x2, g_ref[...]).astype(hn_ref.dtype)


def _merge_kernel(xp_ref, ycp_ref, yrp_ref, pp_ref, xs_ref, ycs_ref, yrs_ref, ps_ref,
                  wo_hbm, wpg_hbm, wple_hbm, g_ref, *refs, last):
    wo_ref, wpg_ref, wple_ref, sem = refs[-4:]
    r = pl.program_id(0)
    n_prompt_steps = pl.num_programs(0) - 1
    if last:
        prompt_outs, sample_outs = refs[0:1], refs[1:2]
    else:
        nw_refs, refs = refs[:4], refs[4:]
        prompt_outs, sample_outs, cw_refs = refs[0:2], refs[2:4], refs[4:8]

    half = ycp_ref.shape[-1]
    weight_copies = [
        pltpu.make_async_copy(wo_hbm.at[0:half], wo_ref.at[0:half], sem.at[0]),
        pltpu.make_async_copy(wo_hbm.at[half:], wo_ref.at[half:], sem.at[1]),
        pltpu.make_async_copy(wpg_hbm, wpg_ref, sem.at[2]),
        pltpu.make_async_copy(wple_hbm, wple_ref, sem.at[3]),
    ]

    def cast_next_weights():
        if not last:
            for src, dst in zip(nw_refs[:3], cw_refs[:3]):
                dst[...] = src[...].astype(dst.dtype)

    @pl.when(r == 0)
    def _():
        for copy in weight_copies:
            copy.start()
        _merge_rows(xp_ref, ycp_ref, yrp_ref, pp_ref, wo_ref, wpg_ref, wple_ref, g_ref, prompt_outs,
                    arrive=weight_copies)
        cast_next_weights()

    @pl.when((r > 0) & (r < n_prompt_steps))
    def _():
        _merge_rows(xp_ref, ycp_ref, yrp_ref, pp_ref, wo_ref, wpg_ref, wple_ref, g_ref, prompt_outs)
        cast_next_weights()

    @pl.when(r == n_prompt_steps)
    def _():
        _merge_rows(xs_ref, ycs_ref, yrs_ref, ps_ref, wo_ref, wpg_ref, wple_ref, g_ref, sample_outs)
        if not last:
            cw_refs[3][...] = nw_refs[3][...].astype(cw_refs[3].dtype)


def _merge_call(layer, xp, ycp, yrp, pp, xs, ycs, yrs, ps, w_out_b, w_pg_b, w_ple_b, g_next,
                next_f32, rows, last):
    n, d = xp.shape
    ns = xs.shape[0]
    half = ycp.shape[-1]
    pd = pp.shape[-1]
    steps = n // rows
    const = dict(pipeline_mode=pl.Buffered(1))

    def prow(r):
        return jnp.minimum(r, steps - 1)

    in_specs = [
        pl.BlockSpec((rows, d), lambda r: (prow(r), 0)),
        pl.BlockSpec((rows, half), lambda r: (prow(r), 0)),
        pl.BlockSpec((rows, half), lambda r: (prow(r), 0)),
        pl.BlockSpec((None, rows, pd), lambda r: (layer, prow(r), 0)),
        pl.BlockSpec((ns, d), lambda r: (0, 0)),
        pl.BlockSpec((ns, half), lambda r: (0, 0)),
        pl.BlockSpec((ns, half), lambda r: (0, 0)),
        pl.BlockSpec((None, ns, pd), lambda r: (layer, 0, 0)),
        pl.BlockSpec(memory_space=pl.ANY),
        pl.BlockSpec(memory_space=pl.ANY),
        pl.BlockSpec(memory_space=pl.ANY),
        (pl.BlockSpec((1, d), lambda r: (0, 0)) if last else
         pl.BlockSpec((None, 1, d), lambda r: (layer + 1, 0, 0))),
    ]
    args = [xp, ycp, yrp, pp, xs, ycs, yrs, ps, w_out_b, w_pg_b, w_ple_b, g_next]
    p_spec = pl.BlockSpec((rows, d), lambda r: (prow(r), 0))
    s_spec = pl.BlockSpec((ns, d), lambda r: (0, 0))
    if last:
        out_specs = [p_spec, s_spec]
        out_shape = [jax.ShapeDtypeStruct((n, d), _F32), jax.ShapeDtypeStruct((ns, d), _F32)]
    else:
        out_specs = [p_spec, p_spec, s_spec, s_spec]
        out_shape = [jax.ShapeDtypeStruct((n, d), _F32), jax.ShapeDtypeStruct((n, d), _BF16),
                     jax.ShapeDtypeStruct((ns, d), _F32), jax.ShapeDtypeStruct((ns, d), _BF16)]
        for w in next_f32[:3]:
            slab = w.shape[1] // steps
            in_specs.append(pl.BlockSpec((None, slab, w.shape[2]), lambda r: (layer + 1, prow(r), 0)))
            out_specs.append(pl.BlockSpec((slab, w.shape[2]), lambda r: (prow(r), 0)))
            out_shape.append(jax.ShapeDtypeStruct(w.shape[1:], _BF16))
        w = next_f32[3]
        in_specs.append(pl.BlockSpec((None,) + w.shape[1:], lambda r: (layer + 1, 0, 0), **const))
        out_specs.append(pl.BlockSpec(w.shape[1:], lambda r: (0, 0)))
        out_shape.append(jax.ShapeDtypeStruct(w.shape[1:], _BF16))
        args += list(next_f32)
    return pl.pallas_call(
        functools.partial(_merge_kernel, last=last),
        grid=(steps + 1,),
        in_specs=in_specs,
        out_specs=out_specs,
        out_shape=out_shape,
        scratch_shapes=[pltpu.VMEM((d, d), _BF16),
                        pltpu.VMEM((d, d), _BF16),
                        pltpu.VMEM((pd, d), _BF16),
                        pltpu.SemaphoreType.DMA((4,))],
        compiler_params=pltpu.CompilerParams(
            dimension_semantics=("arbitrary",), vmem_limit_bytes=_V7X_VMEM_LIMIT_BYTES),
        name="merge_last" if last else "merge",
    )(*args)


def _rope_tables(pos, head_dim):
    half = head_dim // 2
    inv = jnp.power(_ROPE_BASE, -jnp.arange(half, dtype=_F32) / half)
    ang = pos.astype(_F32)[:, None] * inv[None, :]
    return jnp.cos(ang), jnp.sin(ang)


def kernel(x_prompt, x_sample, state_conv, state_ret, p_prompt, p_sample,
           norm_g, w_in, conv_w, gn_g, w_out, w_pg, w_ple, final_g):
    batch, seq, d = x_prompt.shape
    dec_batch, dec_seq, _ = x_sample.shape
    depth = norm_g.shape[0]
    width = conv_w.shape[-1]
    head_dim = width // _RET_HEADS
    assert dec_seq == 1 and state_conv.shape[2] == _CONV_K - 1
    assert seq % _RET_CHUNK == 0

    tm = 1024
    rows = 256
    norm_rows = 1024

    weights_f32 = (w_in, w_out, w_pg, w_ple)
    w_in_b, w_out_b, w_pg_b, w_ple_b = (w[0].astype(_BF16) for w in weights_f32)

    cos_p, sin_p = _rope_tables(jnp.arange(seq, dtype=jnp.int32), head_dim)
    cos_s, sin_s = _rope_tables(_PAST_LEN + jnp.arange(dec_seq, dtype=jnp.int32), head_dim)
    tabs = (cos_p, sin_p, cos_s, sin_s)

    norm_g3 = norm_g.reshape(depth, 1, d)
    gn_g3 = gn_g.reshape(depth, 1, width)
    final_g2 = final_g.reshape(1, d)
    p_prompt2 = p_prompt.reshape(depth, batch * seq, -1)
    p_sample2 = p_sample.reshape(depth, dec_batch * dec_seq, -1)
    state_conv_t = state_conv.transpose(0, 2, 1, 3)

    xp = x_prompt.reshape(batch * seq, d)
    xs = x_sample.reshape(dec_batch * dec_seq, d)
    hp = _norm_call(xp, norm_g3, norm_rows)
    hs = _norm_call(xs, norm_g3, dec_batch)

    conv_p, conv_s = [], []
    nret = None
    for i in range(depth):
        last = i == depth - 1
        g_next = final_g2 if last else norm_g3
        ycp, yrp, nconv_p, nret_p, ycs, yrs, nconv_s, nret_s = _mixer_call(
            i, hp.reshape(batch, seq, d), hs, w_in_b, conv_w, gn_g3, tabs,
            state_conv_t, state_ret, nret, tm)
        nret = (nret_p, nret_s)
        conv_p.append(nconv_p)
        conv_s.append(nconv_s)
        outs = _merge_call(i, xp, ycp.reshape(batch * seq, width), yrp.reshape(batch * seq, width),
                           p_prompt2, xs, ycs, yrs, p_sample2,
                           w_out_b, w_pg_b, w_ple_b, g_next, weights_f32, rows, last)
        if last:
            y_prompt, y_sample = outs
        else:
            xp, hp, xs, hs, w_in_b, w_out_b, w_pg_b, w_ple_b = outs

    return (y_prompt.reshape(batch, seq, d),
            y_sample.reshape(dec_batch, dec_seq, d),
            jnp.stack(conv_p), nret[0],
            jnp.stack(conv_s).transpose(0, 2, 1, 3), nret[1])
```
